```python
import jax, jax.numpy as jnp
from jax import lax
import numpy as np

D_MODEL = 1024
BATCH = 2
SEQ = 8192
DEPTH = 2
DEC_BATCH = 128
DEC_SEQ = 8
PAST_LEN = 2048
PAGE_SIZE = 128

HG_HEADS = 4
HG_KEY = 128
HG_VAL = 128
HG_QK_W = HG_HEADS * HG_KEY
HG_V_W = HG_HEADS * HG_VAL
HG_CHUNK = 64
FOX_HEADS = 8
FOX_HD = 64
FOX_W = FOX_HEADS * FOX_HD
Q_BLOCK = 128
AB_SPLIT_SIZES = (HG_QK_W, HG_QK_W, HG_V_W, HG_V_W, FOX_W, FOX_W, FOX_W, FOX_HEADS)
AB_IN = sum(AB_SPLIT_SIZES)
AB_OUT = HG_V_W + FOX_W
C_GROUPS = 8
C_GROUP_DIM = D_MODEL // C_GROUPS
C_HALF = C_GROUPS * C_GROUP_DIM
C_CHUNK = 128
FF_HIDDEN = 2816
N_AB = (DEPTH + 1) // 2
N_C = DEPTH // 2
ALPHA = (2.0 * DEPTH) ** 0.25
BETA = (8.0 * DEPTH) ** -0.25
NORM_EPS = 1e-5

kernel_name = 'hgrn2_fox_chunkmlp_macaron_deepnorm_step'


def layer_norm(x, g, b):
    xf = x.astype(jnp.float32)
    mu = jnp.mean(xf, axis=-1, keepdims=True)
    var = jnp.mean(jnp.square(xf - mu), axis=-1, keepdims=True)
    return ((xf - mu) * lax.rsqrt(var + NORM_EPS) * g.astype(jnp.float32) + b.astype(jnp.float32)).astype(x.dtype)


def rms_norm(x, g):
    xf = x.astype(jnp.float32)
    return xf * lax.rsqrt(jnp.mean(jnp.square(xf), axis=-1, keepdims=True) + NORM_EPS) * g.astype(jnp.float32)


def swiglu(x, w_gate, w_up, w_down):
    return (jax.nn.silu(x @ w_gate) * (x @ w_up)) @ w_down


def split_cols(h, sizes):
    idx = np.cumsum(sizes)[:-1].tolist()
    return jnp.split(h, idx, axis=-1)


def hgrn2_chunked(q, k, v, log_f, s0):
    B, T, H, K = q.shape
    L = min(HG_CHUNK, T)
    pad = (-T) % L
    padw = ((0, 0), (0, pad), (0, 0), (0, 0))
    q, k, v, log_f = [jnp.pad(a, padw) for a in (q, k, v, log_f)]
    n = (T + pad) // L

    def to_chunks(a):
        return a.reshape(B, n, L, H, a.shape[-1]).transpose(1, 0, 3, 2, 4)

    causal = jnp.tril(jnp.ones((L, L), bool))[:, :, None]

    def step(s, inp):
        qb, kb, vb, gb = inp
        b = jnp.cumsum(gb, axis=2)
        diff = b[:, :, :, None, :] - b[:, :, None, :, :]
        decay = jnp.exp(jnp.where(causal, diff, -jnp.inf))
        attn = jnp.einsum('bhtk,bhsk,bhtsk->bhts', qb, kb, decay)
        o = jnp.einsum('bhts,bhsv->bhtv', attn, vb) + jnp.einsum('bhtk,bhkv->bhtv', qb * jnp.exp(b), s)
        b_last = b[:, :, -1, :]
        s_new = jnp.exp(b_last)[..., None] * s + jnp.einsum('bhsk,bhsv->bhkv', kb * jnp.exp(b_last[:, :, None, :] - b), vb)
        return s_new, o

    s_T, oc = lax.scan(step, s0, tuple(to_chunks(a) for a in (q, k, v, log_f)))
    o = oc.transpose(1, 0, 3, 2, 4).reshape(B, n * L, H, v.shape[-1])[:, :T]
    return o, s_T


def fox_block(qb, fqb, pb, k, v, fk, k_pos):
    s = jnp.einsum('bqhd,bkhd->bhqk', qb, k).astype(jnp.float32) * (FOX_HD ** -0.5)
    s = s + jnp.transpose(fqb, (0, 2, 1))[:, :, :, None] - jnp.transpose(fk, (0, 2, 1))[:, :, None, :]
    s = jnp.where(pb[:, None] >= k_pos[None, :], s, -jnp.inf)
    p = jax.nn.softmax(s, axis=-1)
    return jnp.einsum('bhqk,bkhd->bqhd', p.astype(v.dtype), v)


def fox_attention(q, k, v, fq, fk, q_pos, k_pos):
    B, Tq, H, D = q.shape
    blk = min(Q_BLOCK, Tq)
    pad = (-Tq) % blk
    nb = (Tq + pad) // blk
    q = jnp.pad(q, ((0, 0), (0, pad), (0, 0), (0, 0)))
    fq = jnp.pad(fq, ((0, 0), (0, pad), (0, 0)))
    q_pos = jnp.pad(q_pos, (0, pad), mode='edge')
    qb = q.reshape(B, nb, blk, H, D).transpose(1, 0, 2, 3, 4)
    fqb = fq.reshape(B, nb, blk, H).transpose(1, 0, 2, 3)
    pb = q_pos.reshape(nb, blk)
    o = lax.map(lambda a: fox_block(a[0], a[1], a[2], k, v, fk, k_pos), (qb, fqb, pb))
    return o.transpose(1, 0, 2, 3, 4).reshape(B, nb * blk, H, D)[:, :Tq]


def parallel_ab_mixer(x, s0, past, lb, w_in, norm_g, f_bias, w_out):
    B, T, _ = x.shape
    hq, hf, hi, hgate, fq, fk, fv, ff = split_cols(x @ w_in, AB_SPLIT_SIZES)
    q_h = jax.nn.silu(hq.astype(jnp.float32)).reshape(B, T, HG_HEADS, HG_KEY)
    lb_h = lb.reshape(HG_HEADS, HG_KEY)
    f_gate = lb_h + (1.0 - lb_h) * jax.nn.sigmoid(hf.astype(jnp.float32).reshape(B, T, HG_HEADS, HG_KEY))
    i_h = hi.astype(jnp.float32).reshape(B, T, HG_HEADS, HG_VAL)
    o_h, s_T = hgrn2_chunked(q_h, 1.0 - f_gate, i_h, jnp.log(f_gate), s0.astype(jnp.float32))
    o_h = rms_norm(o_h, norm_g) * jax.nn.silu(hgate.astype(jnp.float32).reshape(B, T, HG_HEADS, HG_VAL))
    o_h = o_h.reshape(B, T, HG_V_W).astype(x.dtype)
    q_f = fq.reshape(B, T, FOX_HEADS, FOX_HD)
    k_f = fk.reshape(B, T, FOX_HEADS, FOX_HD)
    v_f = fv.reshape(B, T, FOX_HEADS, FOX_HD)
    lf = jax.nn.log_sigmoid(ff.astype(jnp.float32) + f_bias.astype(jnp.float32))
    if past is None:
        P = 0
        k_all, v_all, lf_all = k_f, v_f, lf
    else:
        P = past[0].shape[1]
        k_all = jnp.concatenate([past[0].astype(k_f.dtype), k_f], axis=1)
        v_all = jnp.concatenate([past[1].astype(v_f.dtype), v_f], axis=1)
        lf_all = jnp.concatenate([past[2].astype(jnp.float32), lf], axis=1)
    F = jnp.cumsum(lf_all, axis=1)
    q_pos = P + jnp.arange(T)
    k_pos = jnp.arange(P + T)
    o_f = fox_attention(q_f, k_all, v_all, F[:, P:], F, q_pos, k_pos).reshape(B, T, FOX_W).astype(x.dtype)
    out = jnp.concatenate([o_h, o_f], axis=-1) @ w_out
    return out, s_T, k_f, v_f, lf


def chunk_gating_mixer(x, w_in, ln_g, ln_b, w_s, b_s, w_out):
    B, T, _ = x.shape
    u, v = jnp.split(jax.nn.gelu(x @ w_in), 2, axis=-1)
    v = layer_norm(v, ln_g, ln_b)
    pad = (-T) % C_CHUNK
    n = (T + pad) // C_CHUNK
    vc = jnp.pad(v, ((0, 0), (0, pad), (0, 0))).reshape(B, n, C_CHUNK, C_GROUPS, C_GROUP_DIM)
    w = jnp.where(jnp.tril(jnp.ones((C_CHUNK, C_CHUNK), bool)), w_s, 0)
    mixed = jnp.einsum('gts,bnsgc->bntgc', w, vc) + jnp.transpose(b_s)[:, :, None]
    mixed = mixed.reshape(B, n * C_CHUNK, C_HALF)[:, :T]
    return (u * mixed) @ w_out, v


def run_trunk(x, hg_s0, fox_past, prm):
    B = x.shape[0]
    hg_new, k_new, v_new, lf_new, cv_new = [], [], [], [], []
    lb_all = jnp.cumsum(jax.nn.softmax(prm['hg_lb_logits'].astype(jnp.float32), axis=0), axis=0)
    for l in range(DEPTH):
        j = l // 2
        x = layer_norm(ALPHA * x + 0.5 * swiglu(x, prm['ffn_w_gate'][l, 0], prm['ffn_w_up'][l, 0], prm['ffn_w_down'][l, 0]), prm['ln_g'][l, 0], prm['ln_b'][l, 0])
        if l % 2 == 0:
            s0 = jnp.zeros((B, HG_HEADS, HG_KEY, HG_VAL), jnp.float32) if hg_s0 is None else hg_s0[j]
            past = None if fox_past is None else (fox_past[0][j], fox_past[1][j], fox_past[2][j])
            mix, s_T, k_f, v_f, lf = parallel_ab_mixer(x, s0, past, lb_all[l], prm['ab_w_in'][j], prm['hg_norm_g'][j], prm['fox_f_bias'][j], prm['ab_w_out'][j])
            hg_new.append(s_T)
            k_new.append(k_f)
            v_new.append(v_f)
            lf_new.append(lf)
        else:
            mix, v_rows = chunk_gating_mixer(x, prm['c_w_in'][j], prm['c_ln_g'][j], prm['c_ln_b'][j], prm['c_w_s'][j], prm['c_b_s'][j], prm['c_w_out'][j])
            cv_new.append(v_rows)
        x = layer_norm(ALPHA * x + mix, prm['ln_g'][l, 1], prm['ln_b'][l, 1])
        x = layer_norm(ALPHA * x + 0.5 * swiglu(x, prm['ffn_w_gate'][l, 1], prm['ffn_w_up'][l, 1], prm['ffn_w_down'][l, 1]), prm['ln_g'][l, 2], prm['ln_b'][l, 2])
    return x, hg_new, k_new, v_new, lf_new, cv_new


def setup_inputs(seed: int = 0) -> dict:
    key = jax.random.key(seed)
    keys = jax.random.split(key, 24)

    def nrm(i, shape, scale):
        return jax.random.normal(keys[i], shape, jnp.float32) * scale

    n_pages = PAST_LEN // PAGE_SIZE
    n_used = DEC_BATCH * n_pages
    n_phys = (5 * n_used) // 4
    page_table = jax.random.permutation(keys[6], n_phys)[:n_used].reshape(DEC_BATCH, n_pages).astype(jnp.int32)
    return {
        'x_prompt': nrm(0, (BATCH, SEQ, D_MODEL), 1.0),
        'x_sample': nrm(1, (DEC_BATCH, DEC_SEQ, D_MODEL), 1.0),
        'cache_fox_k': nrm(2, (N_AB, n_phys, PAGE_SIZE, FOX_HEADS, FOX_HD), 1.0),
        'cache_fox_v': nrm(3, (N_AB, n_phys, PAGE_SIZE, FOX_HEADS, FOX_HD), 1.0),
        'cache_fox_logf': jax.nn.log_sigmoid(nrm(4, (N_AB, n_phys, PAGE_SIZE, FOX_HEADS), 1.0) + 2.0),
        'state_hg': nrm(5, (N_AB, DEC_BATCH, HG_HEADS, HG_KEY, HG_VAL), 0.5),
        'page_table': page_table,
        'ln_g': 1.0 + nrm(7, (DEPTH, 3, D_MODEL), 0.02),
        'ln_b': nrm(8, (DEPTH, 3, D_MODEL), 0.02),
        'ffn_w_gate': nrm(9, (DEPTH, 2, D_MODEL, FF_HIDDEN), D_MODEL ** -0.5),
        'ffn_w_up': nrm(10, (DEPTH, 2, D_MODEL, FF_HIDDEN), D_MODEL ** -0.5),
        'ffn_w_down': nrm(11, (DEPTH, 2, FF_HIDDEN, D_MODEL), FF_HIDDEN ** -0.5 * BETA),
        'ab_w_in': nrm(12, (N_AB, D_MODEL, AB_IN), D_MODEL ** -0.5),
        'hg_lb_logits': nrm(13, (DEPTH + 1, HG_QK_W), 0.5),
        'hg_norm_g': 1.0 + nrm(14, (N_AB, HG_VAL), 0.02),
        'fox_f_bias': 2.0 + nrm(15, (N_AB, FOX_HEADS), 0.5),
        'ab_w_out': nrm(16, (N_AB, AB_OUT, D_MODEL), AB_OUT ** -0.5 * BETA),
        'c_w_in': nrm(17, (N_C, D_MODEL, 2 * C_HALF), D_MODEL ** -0.5),
        'c_ln_g': 1.0 + nrm(18, (N_C, C_HALF), 0.02),
        'c_ln_b': nrm(19, (N_C, C_HALF), 0.02),
        'c_w_s': nrm(20, (N_C, C_GROUPS, C_CHUNK, C_CHUNK), C_CHUNK ** -0.5),
        'c_b_s': 1.0 + nrm(21, (N_C, C_GROUPS, C_CHUNK), 0.1),
        'c_w_out': nrm(22, (N_C, C_HALF, D_MODEL), C_HALF ** -0.5 * BETA),
    }


def reference(x_prompt, x_sample, cache_fox_k, cache_fox_v, cache_fox_logf, state_hg, page_table,
              ln_g, ln_b, ffn_w_gate, ffn_w_up, ffn_w_down, ab_w_in, hg_lb_logits, hg_norm_g, fox_f_bias,
              ab_w_out, c_w_in, c_ln_g, c_ln_b, c_w_s, c_b_s, c_w_out):
    prm = dict(ln_g=ln_g, ln_b=ln_b, ffn_w_gate=ffn_w_gate, ffn_w_up=ffn_w_up, ffn_w_down=ffn_w_down,
               ab_w_in=ab_w_in, hg_lb_logits=hg_lb_logits, hg_norm_g=hg_norm_g, fox_f_bias=fox_f_bias,
               ab_w_out=ab_w_out, c_w_in=c_w_in, c_ln_g=c_ln_g, c_ln_b=c_ln_b, c_w_s=c_w_s, c_b_s=c_b_s,
               c_w_out=c_w_out)
    y_prompt, hg_p, k_p, v_p, lf_p, _ = run_trunk(x_prompt, None, None, prm)
    dec_b, n_pages = page_table.shape
    past_len = n_pages * PAGE_SIZE
    past_k = cache_fox_k[:, page_table].reshape(N_AB, dec_b, past_len, FOX_HEADS, FOX_HD)
    past_v = cache_fox_v[:, page_table].reshape(N_AB, dec_b, past_len, FOX_HEADS, FOX_HD)
    past_lf = cache_fox_logf[:, page_table].reshape(N_AB, dec_b, past_len, FOX_HEADS)
    y_sample, hg_s, k_s, v_s, lf_s, cv_s = run_trunk(x_sample, state_hg, (past_k, past_v, past_lf), prm)
    return (y_prompt, y_sample,
            jnp.stack(k_p), jnp.stack(v_p), jnp.stack(lf_p), jnp.stack(hg_p).astype(x_prompt.dtype),
            jnp.stack(k_s), jnp.stack(v_s), jnp.stack(lf_s), jnp.stack(hg_s).astype(state_hg.dtype),
            jnp.stack(cv_s))
```

```python
import functools

import jax
import jax.numpy as jnp
import numpy as np
from jax import lax
from jax.experimental import pallas as pl
from jax.experimental.pallas import tpu as pltpu

F32 = jnp.float32
BF16 = jnp.bfloat16
HIGHEST = lax.Precision.HIGHEST

NORM_EPS = 1e-5
LANES = 128
SUBLANES = 8
VMEM_LIMIT_BYTES = 56 * 1024 * 1024

ROW_TILE = 512
FFN_COL_CHUNK = 256
HG_SUB = SUBLANES
HG_CHUNK = 64
HG_ROWS_PER_STEP = 256
FOX_BQ = 1024
FOX_BK = 1024
PAGES_PER_STEP = 4
C_CHUNK = 128

_NT = (((1,), (1,)), ((), ()))
_TN = (((0,), (0,)), ((), ()))


def _cparams(semantics):
    return pltpu.CompilerParams(dimension_semantics=semantics, vmem_limit_bytes=VMEM_LIMIT_BYTES)


def _resident(shape):
    nd = len(shape)
    return pl.BlockSpec(shape, lambda *_: (0,) * nd, pipeline_mode=pl.Buffered(1))


def _rows(tm, width):
    return pl.BlockSpec((tm, width), lambda i: (i, 0))


def _layer_norm(y, g, b):
    mu = jnp.mean(y, axis=-1, keepdims=True)
    d = y - mu
    var = jnp.mean(d * d, axis=-1, keepdims=True)
    return d * lax.rsqrt(var + NORM_EPS) * g + b


def _silu(x):
    return x * jax.nn.sigmoid(x)


def _ffn_ln_kernel(x_ref, wg_ref, wu_ref, wd_ref, g_ref, b_ref, o_ref, a_scr, *, alpha):
    x = x_ref[...]
    xb = x.astype(BF16)
    hidden = wg_ref.shape[1]
    for c in range(hidden // FFN_COL_CHUNK):
        sl = slice(c * FFN_COL_CHUNK, (c + 1) * FFN_COL_CHUNK)
        gate = jnp.dot(xb, wg_ref[:, sl], preferred_element_type=F32)
        up = jnp.dot(xb, wu_ref[:, sl], preferred_element_type=F32)
        a_scr[:, sl] = (_silu(gate) * up).astype(BF16)
    y = jnp.dot(a_scr[...], wd_ref[...], preferred_element_type=F32)
    o_ref[...] = _layer_norm(alpha * x + 0.5 * y, g_ref[...], b_ref[...])


def _ffn_ln(x, wg, wu, wd, g, b, alpha):
    n, d = x.shape
    f = wg.shape[1]
    tm = min(ROW_TILE, n)
    return pl.pallas_call(
        functools.partial(_ffn_ln_kernel, alpha=alpha),
        grid=(n // tm,),
        in_specs=[_rows(tm, d), _resident((d, f)), _resident((d, f)), _resident((f, d)),
                  _resident((1, d)), _resident((1, d))],
        out_specs=_rows(tm, d),
        out_shape=jax.ShapeDtypeStruct((n, d), F32),
        scratch_shapes=[pltpu.VMEM((tm, f), BF16)],
        compiler_params=_cparams(("parallel",)),
        name="ffn_ln",
    )(x, wg, wu, wd, g, b)


def _ab_in_kernel(x_ref, w_ref, wff_ref, lbl_ref, fb_ref,
                  qh_ref, kk_ref, lg_ref, ih_ref, gt_ref, fqb_ref, fkb_ref, fvb_ref, fk_ref, fv_ref,
                  lf_ref, lft_ref, *, layer, hw, fw, fox_heads, fox_scale):
    xb = x_ref[...].astype(BF16)
    offs = np.cumsum([0, hw, hw, hw, hw, fw, fw, fw]).tolist()

    def seg(i):
        return jnp.dot(xb, w_ref[:, offs[i]:offs[i + 1]], preferred_element_type=F32)

    qh_ref[...] = _silu(seg(0))
    z = lbl_ref[...]
    e = jnp.exp(z - jnp.max(z, axis=0, keepdims=True))
    lb = jnp.sum(e[:layer + 1], axis=0, keepdims=True) / jnp.sum(e, axis=0, keepdims=True)
    f_gate = lb + (1.0 - lb) * jax.nn.sigmoid(seg(1))
    kk_ref[...] = 1.0 - f_gate
    lg_ref[...] = jnp.log(f_gate)
    ih_ref[...] = seg(2)
    gt_ref[...] = _silu(seg(3))
    fqb_ref[...] = (seg(4) * fox_scale).astype(BF16)
    fk = seg(5)
    fk_ref[...] = fk
    fkb_ref[...] = fk.astype(BF16)
    fv = seg(6)
    fv_ref[...] = fv
    fvb_ref[...] = fv.astype(BF16)
    ff = jnp.dot(xb, wff_ref[...], preferred_element_type=F32) + fb_ref[...]
    lf = jax.nn.log_sigmoid(ff)
    lf_ref[...] = lf[:, :fox_heads]
    lft_ref[...] = lf.T[:fox_heads, :]


def _ab_in(x, w_main, w_ff, lb_logits, f_bias, layer, hw, fw, fox_heads, fox_scale):
    n, d = x.shape
    tm = min(ROW_TILE, n)
    f32_out = lambda w: jax.ShapeDtypeStruct((n, w), F32)
    bf_out = lambda w: jax.ShapeDtypeStruct((n, w), BF16)
    out_shape = ([f32_out(hw)] * 5 + [bf_out(fw)] * 3 + [f32_out(fw)] * 2
                 + [f32_out(fox_heads), jax.ShapeDtypeStruct((fox_heads, n), F32)])
    out_specs = ([_rows(tm, hw)] * 5 + [_rows(tm, fw)] * 5
                 + [_rows(tm, fox_heads), pl.BlockSpec((fox_heads, tm), lambda i: (0, i))])
    return pl.pallas_call(
        functools.partial(_ab_in_kernel, layer=layer, hw=hw, fw=fw, fox_heads=fox_heads,
                          fox_scale=fox_scale),
        grid=(n // tm,),
        in_specs=[_rows(tm, d), _resident(w_main.shape), _resident(w_ff.shape),
                  _resident(lb_logits.shape), _resident(f_bias.shape)],
        out_specs=out_specs,
        out_shape=out_shape,
        compiler_params=_cparams(("parallel",)),
        name="ab_in",
    )(x, w_main, w_ff, lb_logits, f_bias)


def _hgrn_levels(chunk):
    levels, h = [], HG_SUB
    while h < chunk:
        levels.append(h)
        h *= 2
    return levels


def _hgrn_prefix_matrix(chunk):
    t = lax.broadcasted_iota(jnp.int32, (chunk, chunk), 0)
    s = lax.broadcasted_iota(jnp.int32, (chunk, chunk), 1)
    parts = [s <= t]
    for h in _hgrn_levels(chunk):
        ref_row = (t // (2 * h)) * (2 * h) + h - 1
        parts.append(s <= ref_row)
    return jnp.concatenate(parts, axis=0).astype(F32)


def _hgrn_diag(q, k, b, v):
    rows, dk = q.shape
    nb = rows // HG_SUB
    q3, k3, b3 = (a.reshape(nb, HG_SUB, dk) for a in (q, k, b))
    v3 = v.reshape(nb, HG_SUB, v.shape[-1])
    t_loc = lax.broadcasted_iota(jnp.int32, (1, HG_SUB, 1), 1)
    o3 = jnp.zeros(v3.shape, F32)
    for s in range(HG_SUB):
        decay = jnp.exp(jnp.where(t_loc >= s, b3 - b3[:, s:s + 1, :], -jnp.inf))
        a = jnp.sum(q3 * decay * k3[:, s:s + 1, :], axis=-1, keepdims=True)
        o3 = o3 + a * v3[:, s:s + 1, :]
    return o3.reshape(rows, v.shape[-1])


def _hgrn_kernel(*refs, chunk, heads, dk, dv, has_s0):
    if has_s0:
        q_ref, k_ref, g_ref, v_ref, gt_ref, ng_ref, s0_ref, o_ref, st_ref, s_scr = refs
    else:
        q_ref, k_ref, g_ref, v_ref, gt_ref, ng_ref, o_ref, st_ref, s_scr = refs
    tb = pl.program_id(1)

    @pl.when(tb == 0)
    def _():
        s_scr[...] = s0_ref[...] if has_s0 else jnp.zeros(s_scr.shape, F32)

    levels = _hgrn_levels(chunk)
    prefix = _hgrn_prefix_matrix(chunk)
    t = lax.broadcasted_iota(jnp.int32, (chunk, chunk), 0)
    s = lax.broadcasted_iota(jnp.int32, (chunk, chunk), 1)
    masks = [(t // (2 * h) == s // (2 * h)) & (t % (2 * h) >= h) & (s % (2 * h) < h) for h in levels]
    ng = ng_ref[...]

    for c in range(q_ref.shape[0] // chunk):
        rows = slice(c * chunk, (c + 1) * chunk)
        q, k, v = q_ref[rows, :], k_ref[rows, :], v_ref[rows, :]
        stacked = jnp.dot(prefix, g_ref[rows, :], precision=HIGHEST, preferred_element_type=F32)
        b = stacked[:chunk]
        b_last = b[chunk - 1:chunk, :]
        q_in = (q * jnp.exp(b)).astype(BF16)
        k_out = (k * jnp.exp(b_last - b)).astype(BF16)
        vb = v.astype(BF16)
        q_lv, k_lv = [], []
        for j in range(len(levels)):
            w = jnp.exp(-jnp.abs(b - stacked[(j + 1) * chunk:(j + 2) * chunk]))
            q_lv.append((q * w).astype(BF16))
            k_lv.append((k * w).astype(BF16))
        for hd in range(heads):
            ks = slice(hd * dk, (hd + 1) * dk)
            vs = slice(hd * dv, (hd + 1) * dv)
            state = s_scr[hd]
            o = jnp.dot(q_in[:, ks], state.astype(BF16), preferred_element_type=F32)
            o = o + _hgrn_diag(q[:, ks], k[:, ks], b[:, ks], v[:, vs])
            if levels:
                attn = jnp.zeros((chunk, chunk), F32)
                for j in range(len(levels)):
                    blk = lax.dot_general(q_lv[j][:, ks], k_lv[j][:, ks], _NT, preferred_element_type=F32)
                    attn = attn + jnp.where(masks[j], blk, 0.0)
                o = o + jnp.dot(attn.astype(BF16), vb[:, vs], preferred_element_type=F32)
            decay_col = jnp.broadcast_to(jnp.exp(b_last[:, ks]), (dv, dk)).T
            s_scr[hd] = decay_col * state + lax.dot_general(k_out[:, ks], vb[:, vs], _TN,
                                                            preferred_element_type=F32)
            o = o * lax.rsqrt(jnp.mean(o * o, axis=-1, keepdims=True) + NORM_EPS) * ng * gt_ref[rows, vs]
            o_ref[rows, vs] = o.astype(o_ref.dtype)

    @pl.when(tb == pl.num_programs(1) - 1)
    def _():
        st_ref[...] = s_scr[...]


def _hgrn(q, k, g, v, gate, norm_g, s0, *, chunk, rows_per_step, heads, dk, dv):
    nb, t, _ = q.shape
    tb = min(rows_per_step, t)
    seq = lambda w: pl.BlockSpec((None, tb, w), lambda b, i: (b, i, 0))
    st_spec = pl.BlockSpec((None, heads, dk, dv), lambda b, i: (b, 0, 0, 0))
    in_specs = [seq(heads * dk)] * 3 + [seq(heads * dv)] * 2 + [_resident(norm_g.shape)]
    args = [q, k, g, v, gate, norm_g]
    if s0 is not None:
        in_specs.append(st_spec)
        args.append(s0)
    return pl.pallas_call(
        functools.partial(_hgrn_kernel, chunk=chunk, heads=heads, dk=dk, dv=dv, has_s0=s0 is not None),
        grid=(nb, t // tb),
        in_specs=in_specs,
        out_specs=[seq(heads * dv), st_spec],
        out_shape=[jax.ShapeDtypeStruct((nb, t, heads * dv), BF16),
                   jax.ShapeDtypeStruct((nb, heads, dk, dv), F32)],
        scratch_shapes=[pltpu.VMEM((heads, dk, dv), F32)],
        compiler_params=_cparams(("parallel", "arbitrary")),
        name="hgrn_state" if s0 is not None else "hgrn_fresh",
    )(*args)


def _lf_cumsum_kernel(lft_ref, o_ref, x_scr, *, blocks_per_seq):
    heads, n = lft_ref.shape
    nblk = n // LANES
    for i in range(nblk):
        x_scr[i * heads:(i + 1) * heads, :] = lft_ref[:, i * LANES:(i + 1) * LANES]
    r = lax.broadcasted_iota(jnp.int32, (LANES, LANES), 0)
    c = lax.broadcasted_iota(jnp.int32, (LANES, LANES), 1)
    local = jnp.dot(x_scr[...], (r <= c).astype(F32), precision=HIGHEST, preferred_element_type=F32)
    rows = nblk * heads
    ri = lax.broadcasted_iota(jnp.int32, (rows, rows), 0)
    ci = lax.broadcasted_iota(jnp.int32, (rows, rows), 1)
    earlier = ((ri % heads == ci % heads) & (ci // heads < ri // heads)
               & (ci // (heads * blocks_per_seq) == ri // (heads * blocks_per_seq)))
    carry = jnp.dot(earlier.astype(F32), local, precision=HIGHEST, preferred_element_type=F32)
    x_scr[...] = local + carry[:, LANES - 1:LANES]
    for i in range(nblk):
        o_ref[:, i * LANES:(i + 1) * LANES] = x_scr[i * heads:(i + 1) * heads, :]


def _lf_cumsum(lft, seq_len):
    heads, n = lft.shape
    return pl.pallas_call(
        functools.partial(_lf_cumsum_kernel, blocks_per_seq=seq_len // LANES),
        out_shape=jax.ShapeDtypeStruct((heads, n), F32),
        scratch_shapes=[pltpu.VMEM((n // LANES * heads, LANES), F32)],
        compiler_params=pltpu.CompilerParams(vmem_limit_bytes=VMEM_LIMIT_BYTES),
        name="lf_cumsum",
    )(lft)


def _fox_prompt_kernel(qi_ref, ki_ref, q_ref, k_ref, v_ref, fk_ref, fq_ref, o_ref,
                       qm_scr, m_scr, l_scr, acc_scr, *, bq, bk, hd):
    step = pl.program_id(2)
    qi = qi_ref[step]
    ki = ki_ref[step]
    lane = lax.broadcasted_iota(jnp.int32, (1, 2 * hd), 1)

    @pl.when(ki == 0)
    def _():
        q = q_ref[...]
        for e in range(2):
            qm_scr[e] = jnp.where(lane // hd == e, q, jnp.zeros_like(q))
        m_scr[...] = jnp.full(m_scr.shape, -jnp.inf, F32)
        l_scr[...] = jnp.zeros(l_scr.shape, F32)
        acc_scr[...] = jnp.zeros(acc_scr.shape, F32)

    def update(masked):
        kb = k_ref[...]
        vb = v_ref[...]
        if masked:
            row = lax.broadcasted_iota(jnp.int32, (bq, bk), 0)
            col = lax.broadcasted_iota(jnp.int32, (bq, bk), 1)
            visible = (col - row) <= (qi * bq - ki * bk)
        for e in range(2):
            bias = fq_ref[e:e + 1, 0:1] - fk_ref[e:e + 1, :]
            s = lax.dot_general(qm_scr[e], kb, _NT, preferred_element_type=F32) + bias
            if masked:
                s = jnp.where(visible, s, -jnp.inf)
            m_prev = m_scr[e]
            m_new = jnp.maximum(m_prev, jnp.max(s, axis=-1, keepdims=True))
            alpha = jnp.exp(m_prev - m_new)
            p = jnp.exp(s - m_new)
            l_scr[e] = alpha * l_scr[e] + jnp.sum(p, axis=-1, keepdims=True)
            acc_scr[e] = alpha * acc_scr[e] + jnp.dot(p.astype(BF16), vb, preferred_element_type=F32)
            m_scr[e] = m_new

    overlaps_diagonal = (ki + 1) * bk - 1 > qi * bq

    @pl.when(overlaps_diagonal)
    def _():
        update(True)

    @pl.when(jnp.logical_not(overlaps_diagonal))
    def _():
        update(False)

    @pl.when(ki == ((qi + 1) * bq - 1) // bk)
    def _():
        o0 = acc_scr[0] / l_scr[0]
        o1 = acc_scr[1] / l_scr[1]
        o_ref[...] = jnp.where(lane // hd == 0, o0, o1).astype(o_ref.dtype)


def _fox_prompt(q, k, v, fc, *, heads, hd):
    nb, t, w = q.shape
    bq, bk = min(FOX_BQ, t), min(FOX_BK, t)
    pairs = [(i, j) for i in range(t // bq) for j in range(((i + 1) * bq - 1) // bk + 1)]
    qi_tab = jnp.asarray([p[0] for p in pairs], jnp.int32)
    ki_tab = jnp.asarray([p[1] for p in pairs], jnp.int32)
    nqb, nkb = t // bq, t // bk
    fc_pairs = fc.reshape(heads // 2, 2, nb * t)
    grid_spec = pltpu.PrefetchScalarGridSpec(
        num_scalar_prefetch=2,
        grid=(nb, heads // 2, len(pairs)),
        in_specs=[
            pl.BlockSpec((None, bq, 2 * hd), lambda b, hp, s, qi, ki: (b, qi[s], hp)),
            pl.BlockSpec((None, bk, 2 * hd), lambda b, hp, s, qi, ki: (b, ki[s], hp)),
            pl.BlockSpec((None, bk, 2 * hd), lambda b, hp, s, qi, ki: (b, ki[s], hp)),
            pl.BlockSpec((None, 2, bk), lambda b, hp, s, qi, ki: (hp, 0, b * nkb + ki[s])),
            pl.BlockSpec((None, 2, bq), lambda b, hp, s, qi, ki: (hp, 0, b * nqb + qi[s])),
        ],
        out_specs=pl.BlockSpec((None, bq, 2 * hd), lambda b, hp, s, qi, ki: (b, qi[s], hp)),
        scratch_shapes=[pltpu.VMEM((2, bq, 2 * hd), BF16), pltpu.VMEM((2, bq, 1), F32),
                        pltpu.VMEM((2, bq, 1), F32), pltpu.VMEM((2, bq, 2 * hd), F32)],
    )
    return pl.pallas_call(
        functools.partial(_fox_prompt_kernel, bq=bq, bk=bk, hd=hd),
        grid_spec=grid_spec,
        out_shape=jax.ShapeDtypeStruct((nb, t, w), BF16),
        compiler_params=_cparams(("parallel", "parallel", "arbitrary")),
        name="fox_prompt",
    )(qi_tab, ki_tab, q, k, v, fc_pairs, fc_pairs)


def _fox_sample_kernel(*refs, pages_per_step, heads, hd, page):
    pp = pages_per_step
    pt_ref, q_ref, kn_ref, vn_ref, lfn_ref = refs[:5]
    kc_refs = refs[5:5 + pp]
    vc_refs = refs[5 + pp:5 + 2 * pp]
    lc_refs = refs[5 + 2 * pp:5 + 3 * pp]
    o_ref, qbd_scr, m_scr, l_scr, acc_scr, carry_scr, kpad_scr, vpad_scr, lpad_scr = refs[5 + 3 * pp:]
    del pt_ref
    tq = q_ref.shape[0]
    nrow = heads * tq
    w = heads * hd
    pg = pl.program_id(1)
    row_head = lax.broadcasted_iota(jnp.int32, (nrow, w), 0) // tq
    col_head = lax.broadcasted_iota(jnp.int32, (nrow, w), 1) // hd

    @pl.when(pg == 0)
    def _():
        q = q_ref[...].astype(F32)
        q_rows = jnp.concatenate([q] * heads, axis=0)
        qbd_scr[...] = jnp.where(row_head == col_head, q_rows, 0.0).astype(BF16)
        m_scr[...] = jnp.full(m_scr.shape, -jnp.inf, F32)
        l_scr[...] = jnp.zeros(l_scr.shape, F32)
        acc_scr[...] = jnp.zeros(acc_scr.shape, F32)
        carry_scr[...] = jnp.zeros(carry_scr.shape, F32)

    expand = (lax.broadcasted_iota(jnp.int32, (nrow, heads), 0) // tq
              == lax.broadcasted_iota(jnp.int32, (nrow, heads), 1)).astype(F32)
    upper = (lax.broadcasted_iota(jnp.int32, (page, page), 0)
             <= lax.broadcasted_iota(jnp.int32, (page, page), 1)).astype(F32)

    def attend(kb, vb, lf, visible):
        s = lax.dot_general(qbd_scr[...], kb, _NT, preferred_element_type=F32)
        lf_rows = lax.dot_general(expand, lf, _NT, precision=HIGHEST, preferred_element_type=F32)
        cum = jnp.dot(lf_rows, upper, precision=HIGHEST, preferred_element_type=F32)
        carry = carry_scr[...]
        s = s - (carry + cum)
        if visible is not None:
            s = jnp.where(visible, s, -jnp.inf)
        carry_scr[...] = carry + cum[:, page - 1:page]
        m_prev = m_scr[...]
        m_new = jnp.maximum(m_prev, jnp.max(s, axis=-1, keepdims=True))
        alpha = jnp.exp(m_prev - m_new)
        p = jnp.exp(s - m_new)
        l_scr[...] = alpha * l_scr[...] + jnp.sum(p, axis=-1, keepdims=True)
        acc_scr[...] = alpha * acc_scr[...] + jnp.dot(p.astype(BF16), vb, preferred_element_type=F32)
        m_scr[...] = m_new

    for j in range(pp):
        attend(kc_refs[j][...].astype(BF16), vc_refs[j][...].astype(BF16), lc_refs[j][...], None)

    @pl.when(pg == pl.num_programs(1) - 1)
    def _():
        kpad_scr[...] = jnp.zeros(kpad_scr.shape, F32)
        vpad_scr[...] = jnp.zeros(vpad_scr.shape, F32)
        lpad_scr[...] = jnp.zeros(lpad_scr.shape, F32)
        kpad_scr[0:tq, :] = kn_ref[...]
        vpad_scr[0:tq, :] = vn_ref[...]
        lpad_scr[0:tq, :] = lfn_ref[...]
        t_q = lax.broadcasted_iota(jnp.int32, (nrow, page), 0) % tq
        s_k = lax.broadcasted_iota(jnp.int32, (nrow, page), 1)
        attend(kpad_scr[...].astype(BF16), vpad_scr[...].astype(BF16), lpad_scr[...],
               (s_k < tq) & (s_k <= t_q))
        o = acc_scr[...] / l_scr[...]
        o = jnp.where(row_head == col_head, o, 0.0)
        o_ref[...] = jnp.sum(o.reshape(heads, tq, w), axis=0).astype(o_ref.dtype)


def _fox_sample(q, k_new, v_new, lf_new, cache_k, cache_v, cache_lf, page_table, *, heads, hd):
    nb, tq, w = q.shape
    n_pages = page_table.shape[1]
    page = cache_k.shape[1]
    pp = PAGES_PER_STEP
    while n_pages % pp:
        pp //= 2
    new_spec = lambda width: pl.BlockSpec((None, tq, width), lambda b, g, pt: (b, 0, 0))

    def cache_spec(width, j):
        return pl.BlockSpec((None, page, width), lambda b, g, pt: (pt[b * n_pages + g * pp + j], 0, 0))

    grid_spec = pltpu.PrefetchScalarGridSpec(
        num_scalar_prefetch=1,
        grid=(nb, n_pages // pp),
        in_specs=([new_spec(w)] * 3 + [new_spec(heads)]
                  + [cache_spec(w, j) for j in range(pp)] + [cache_spec(w, j) for j in range(pp)]
                  + [cache_spec(heads, j) for j in range(pp)]),
        out_specs=new_spec(w),
        scratch_shapes=[pltpu.VMEM((heads * tq, w), BF16), pltpu.VMEM((heads * tq, 1), F32),
                        pltpu.VMEM((heads * tq, 1), F32), pltpu.VMEM((heads * tq, w), F32),
                        pltpu.VMEM((heads * tq, 1), F32), pltpu.VMEM((page, w), F32),
                        pltpu.VMEM((page, w), F32), pltpu.VMEM((page, heads), F32)],
    )
    return pl.pallas_call(
        functools.partial(_fox_sample_kernel, pages_per_step=pp, heads=heads, hd=hd, page=page),
        grid_spec=grid_spec,
        out_shape=jax.ShapeDtypeStruct((nb, tq, w), BF16),
        compiler_params=_cparams(("parallel", "arbitrary")),
        name="fox_sample",
    )(page_table.reshape(-1), q, k_new, v_new, lf_new,
      *([cache_k] * pp), *([cache_v] * pp), *([cache_lf] * pp))


def _ab_out_ln_kernel(x_ref, oh_ref, of_ref, wh_ref, wf_ref, g_ref, b_ref, o_ref, *, alpha):
    y = (jnp.dot(oh_ref[...], wh_ref[...], preferred_element_type=F32)
         + jnp.dot(of_ref[...], wf_ref[...], preferred_element_type=F32))
    o_ref[...] = _layer_norm(alpha * x_ref[...] + y, g_ref[...], b_ref[...])


def _ab_out_ln(x, oh, of, wh, wf, g, b, alpha):
    n, d = x.shape
    tm = min(ROW_TILE, n)
    return pl.pallas_call(
        functools.partial(_ab_out_ln_kernel, alpha=alpha),
        grid=(n // tm,),
        in_specs=[_rows(tm, d), _rows(tm, oh.shape[1]), _rows(tm, of.shape[1]), _resident(wh.shape),
                  _resident(wf.shape), _resident((1, d)), _resident((1, d))],
        out_specs=_rows(tm, d),
        out_shape=jax.ShapeDtypeStruct((n, d), F32),
        compiler_params=_cparams(("parallel",)),
        name="ab_out_ln",
    )(x, oh, of, wh, wf, g, b)


def _gmlp_in_kernel(x_ref, w_ref, g_ref, b_ref, u_ref, v_ref):
    xb = x_ref[...].astype(BF16)
    half = u_ref.shape[1]
    u_ref[...] = jax.nn.gelu(jnp.dot(xb, w_ref[:, :half], preferred_element_type=F32))
    v = jax.nn.gelu(jnp.dot(xb, w_ref[:, half:], preferred_element_type=F32))
    v_ref[...] = _layer_norm(v, g_ref[...], b_ref[...])


def _gmlp_in(x, w, g, b):
    n, d = x.shape
    half = w.shape[1] // 2
    tm = min(ROW_TILE, n)
    return pl.pallas_call(
        _gmlp_in_kernel,
        grid=(n // tm,),
        in_specs=[_rows(tm, d), _resident(w.shape), _resident((1, half)), _resident((1, half))],
        out_specs=[_rows(tm, half), _rows(tm, half)],
        out_shape=[jax.ShapeDtypeStruct((n, half), F32)] * 2,
        compiler_params=_cparams(("parallel",)),
        name="gmlp_in",
    )(x, w, g, b)


def _gmlp_mix_out_ln_kernel(x_ref, u_ref, v_ref, ws_ref, bs_ref, wo_ref, g_ref, b_ref, o_ref, z_scr,
                            *, alpha, period):
    groups, ch, _ = ws_ref.shape
    gw = v_ref.shape[1] // groups
    t = lax.broadcasted_iota(jnp.int32, (ch, ch), 0)
    s = lax.broadcasted_iota(jnp.int32, (ch, ch), 1)
    causal = (t // period == s // period) & (s % period <= t % period)
    for g in range(groups):
        w_g = jnp.where(causal, ws_ref[g], 0.0).astype(BF16)
        cols = slice(g * gw, (g + 1) * gw)
        for c in range(x_ref.shape[0] // ch):
            rows = slice(c * ch, (c + 1) * ch)
            mixed = (jnp.dot(w_g, v_ref[rows, cols].astype(BF16), preferred_element_type=F32)
                     + bs_ref[:, g:g + 1])
            z_scr[rows, cols] = (u_ref[rows, cols] * mixed).astype(BF16)
    y = jnp.dot(z_scr[...], wo_ref[...], preferred_element_type=F32)
    o_ref[...] = _layer_norm(alpha * x_ref[...] + y, g_ref[...], b_ref[...])


def _gmlp_mix_out_ln(x, u, v, w_s, b_s_t, w_out, g, b, alpha, period):
    n, d = x.shape
    half = u.shape[1]
    tm = min(ROW_TILE, n)
    return pl.pallas_call(
        functools.partial(_gmlp_mix_out_ln_kernel, alpha=alpha, period=period),
        grid=(n // tm,),
        in_specs=[_rows(tm, d), _rows(tm, half), _rows(tm, half), _resident(w_s.shape),
                  _resident(b_s_t.shape), _resident(w_out.shape), _resident((1, d)), _resident((1, d))],
        out_specs=_rows(tm, d),
        out_shape=jax.ShapeDtypeStruct((n, d), F32),
        scratch_shapes=[pltpu.VMEM((tm, half), BF16)],
        compiler_params=_cparams(("parallel",)),
        name="gmlp_mix_out_ln",
    )(x, u, v, w_s, b_s_t, w_out, g, b)


def kernel(x_prompt, x_sample, cache_fox_k, cache_fox_v, cache_fox_logf, state_hg, page_table, ln_g, ln_b,
           ffn_w_gate, ffn_w_up, ffn_w_down, ab_w_in, hg_lb_logits, hg_norm_g, fox_f_bias, ab_w_out, c_w_in,
           c_ln_g, c_ln_b, c_w_s, c_b_s, c_w_out):
    nb_p, t_p, d = x_prompt.shape
    nb_s, t_s, _ = x_sample.shape
    depth = ln_g.shape[0]
    alpha = (2.0 * depth) ** 0.25
    hg_heads, hg_k, hg_v = state_hg.shape[2:]
    fox_heads, fox_hd = cache_fox_k.shape[3:]
    hw, fw = hg_heads * hg_k, fox_heads * fox_hd
    assert hg_k == hg_v and hw == hg_heads * hg_v
    n_phys, page = cache_fox_k.shape[1:3]

    xp = x_prompt.reshape(nb_p * t_p, d)
    xs = x_sample.reshape(nb_s * t_s, d)
    row = lambda a: a.reshape(1, -1)
    outs = {name: [] for name in ("k_p", "v_p", "lf_p", "hg_p", "k_s", "v_s", "lf_s", "hg_s", "cv_s")}

    for l in range(depth):
        j = l // 2

        def half_step(x, i):
            return _ffn_ln(x, ffn_w_gate[l, i].astype(BF16), ffn_w_up[l, i].astype(BF16),
                           ffn_w_down[l, i].astype(BF16), row(ln_g[l, i]), row(ln_b[l, i]), alpha)

        xp, xs = half_step(xp, 0), half_step(xs, 0)
        if l % 2 == 0:
            main_w = 4 * hw + 3 * fw
            w_main = ab_w_in[j, :, :main_w].astype(BF16)
            w_ff = jnp.pad(ab_w_in[j, :, main_w:], ((0, 0), (0, LANES - fox_heads))).astype(BF16)
            f_bias = jnp.pad(fox_f_bias[j], (0, LANES - fox_heads)).reshape(1, LANES)
            wh = ab_w_out[j, :hw].astype(BF16)
            wf = ab_w_out[j, hw:].astype(BF16)
            norm_g = row(hg_norm_g[j])

            def project(x):
                return _ab_in(x, w_main, w_ff, hg_lb_logits, f_bias, l, hw, fw, fox_heads, fox_hd ** -0.5)

            qh, kk, lg, ih, gt, fqb, fkb, fvb, fk, fv, lf, lft = project(xp)
            seq = lambda a: a.reshape(nb_p, t_p, a.shape[-1])
            o_h, s_t = _hgrn(seq(qh), seq(kk), seq(lg), seq(ih), seq(gt), norm_g, None, chunk=HG_CHUNK,
                             rows_per_step=HG_ROWS_PER_STEP, heads=hg_heads, dk=hg_k, dv=hg_v)
            fc = _lf_cumsum(lft, t_p)
            o_f = _fox_prompt(seq(fqb), seq(fkb), seq(fvb), fc, heads=fox_heads, hd=fox_hd)
            xp = _ab_out_ln(xp, o_h.reshape(-1, hw), o_f.reshape(-1, fw), wh, wf,
                            row(ln_g[l, 1]), row(ln_b[l, 1]), alpha)
            outs["k_p"].append(fk.reshape(nb_p, t_p, fox_heads, fox_hd))
            outs["v_p"].append(fv.reshape(nb_p, t_p, fox_heads, fox_hd))
            outs["lf_p"].append(lf.reshape(nb_p, t_p, fox_heads))
            outs["hg_p"].append(s_t)

            qh, kk, lg, ih, gt, fqb, fkb, fvb, fk, fv, lf, lft = project(xs)
            seq = lambda a: a.reshape(nb_s, t_s, a.shape[-1])
            o_h, s_t = _hgrn(seq(qh), seq(kk), seq(lg), seq(ih), seq(gt), norm_g, state_hg[j], chunk=t_s,
                             rows_per_step=t_s, heads=hg_heads, dk=hg_k, dv=hg_v)
            o_f = _fox_sample(seq(fqb), seq(fk), seq(fv), seq(lf),
                              cache_fox_k[j].reshape(n_phys, page, fw), cache_fox_v[j].reshape(n_phys, page, fw),
                              cache_fox_logf[j], page_table, heads=fox_heads, hd=fox_hd)
            xs = _ab_out_ln(xs, o_h.reshape(-1, hw), o_f.reshape(-1, fw), wh, wf,
                            row(ln_g[l, 1]), row(ln_b[l, 1]), alpha)
            outs["k_s"].append(fk.reshape(nb_s, t_s, fox_heads, fox_hd))
            outs["v_s"].append(fv.reshape(nb_s, t_s, fox_heads, fox_hd))
            outs["lf_s"].append(lf.reshape(nb_s, t_s, fox_heads))
            outs["hg_s"].append(s_t)
        else:
            w_in = c_w_in[j].astype(BF16)
            w_out = c_w_out[j].astype(BF16)
            reps = C_CHUNK // t_s
            w_s_sample = jnp.tile(c_w_s[j][:, :t_s, :t_s], (1, reps, reps))
            b_s_sample = jnp.tile(c_b_s[j][:, :t_s], (1, reps))

            def mixer(x, w_s, b_s, period):
                u, v = _gmlp_in(x, w_in, row(c_ln_g[j]), row(c_ln_b[j]))
                y = _gmlp_mix_out_ln(x, u, v, w_s, b_s.T, w_out, row(ln_g[l, 1]), row(ln_b[l, 1]),
                                     alpha, period)
                return y, v

            xp, _ = mixer(xp, c_w_s[j], c_b_s[j], C_CHUNK)
            xs, v_rows = mixer(xs, w_s_sample, b_s_sample, t_s)
            outs["cv_s"].append(v_rows.reshape(nb_s, t_s, -1))
        xp, xs = half_step(xp, 2), half_step(xs, 2)

    stack = lambda name: jnp.stack(outs[name])
    return (xp.reshape(nb_p, t_p, d), xs.reshape(nb_s, t_s, d),
            stack("k_p"), stack("v_p"), stack("lf_p"), stack("hg_p"),
            stack("k_s"), stack("v_s"), stack("lf_s"), stack("hg_s"), stack("cv_s"))
```

```python
import functools

import jax
import jax.numpy as jnp
import numpy as np
from jax import lax
from jax.experimental import pallas as pl
from jax.experimental.pallas import tpu as pltpu

F32 = jnp.float32
BF16 = jnp.bfloat16
HIGHEST = lax.Precision.HIGHEST

NORM_EPS = 1e-5
LANES = 128
SUBLANES = 8
VMEM_LIMIT_BYTES = 56 * 1024 * 1024

ROW_TILE = 512
FFN_COL_CHUNK = 256
HG_SUB = SUBLANES
HG_CHUNK = 64
HG_ROWS_PER_STEP = 256
FOX_BQ = 1024
FOX_BK = 1024
FOX_SAMPLE_CHUNK_PAGES = 2
C_CHUNK = 128

_NT = (((1,), (1,)), ((), ()))
_TN = (((0,), (0,)), ((), ()))


def _cparams(semantics):
    return pltpu.CompilerParams(dimension_semantics=semantics, vmem_limit_bytes=VMEM_LIMIT_BYTES)


def _resident(shape):
    nd = len(shape)
    return pl.BlockSpec(shape, lambda *_: (0,) * nd, pipeline_mode=pl.Buffered(1))


def _rows(tm, width):
    return pl.BlockSpec((tm, width), lambda i: (i, 0))


def _layer_norm(y, g, b):
    mu = jnp.mean(y, axis=-1, keepdims=True)
    d = y - mu
    var = jnp.mean(d * d, axis=-1, keepdims=True)
    return d * lax.rsqrt(var + NORM_EPS) * g + b


def _silu(x):
    return x * jax.nn.sigmoid(x)


def _split_bf16(x, pieces=3):
    out = []
    for _ in range(pieces):
        piece = x.astype(BF16)
        out.append(piece)
        x = x - piece.astype(F32)
    return out


def _ffn_ln_kernel(x_ref, wg_ref, wu_ref, wd_ref, g_ref, b_ref, o_ref, a_scr, *, alpha):
    x = x_ref[...]
    xb = x.astype(BF16)
    hidden = wg_ref.shape[1]
    for c in range(hidden // FFN_COL_CHUNK):
        sl = slice(c * FFN_COL_CHUNK, (c + 1) * FFN_COL_CHUNK)
        gate = jnp.dot(xb, wg_ref[:, sl], preferred_element_type=F32)
        up = jnp.dot(xb, wu_ref[:, sl], preferred_element_type=F32)
        a_scr[:, sl] = (_silu(gate) * up).astype(BF16)
    y = jnp.dot(a_scr[...], wd_ref[...], preferred_element_type=F32)
    o_ref[...] = _layer_norm(alpha * x + 0.5 * y, g_ref[...], b_ref[...])


def _ffn_ln(x, wg, wu, wd, g, b, alpha):
    n, d = x.shape
    f = wg.shape[1]
    tm = min(ROW_TILE, n)
    return pl.pallas_call(
        functools.partial(_ffn_ln_kernel, alpha=alpha),
        grid=(n // tm,),
        in_specs=[_rows(tm, d), _resident((d, f)), _resident((d, f)), _resident((f, d)),
                  _resident((1, d)), _resident((1, d))],
        out_specs=_rows(tm, d),
        out_shape=jax.ShapeDtypeStruct((n, d), F32),
        scratch_shapes=[pltpu.VMEM((tm, f), BF16)],
        compiler_params=_cparams(("parallel",)),
        name="ffn_ln",
    )(x, wg, wu, wd, g, b)


def _ab_in_kernel(x_ref, w_ref, wff_ref, lbl_ref, fb_ref,
                  qh_ref, kk_ref, lg_ref, ih_ref, gt_ref, fqb_ref, fkb_ref, fvb_ref, fk_ref, fv_ref,
                  lf_ref, lft_ref, *, layer, hw, fw, fox_heads, fox_scale):
    xb = x_ref[...].astype(BF16)
    offs = np.cumsum([0, hw, hw, hw, hw, fw, fw, fw]).tolist()

    def seg(i):
        return jnp.dot(xb, w_ref[:, offs[i]:offs[i + 1]], preferred_element_type=F32)

    qh_ref[...] = _silu(seg(0))
    z = lbl_ref[...]
    e = jnp.exp(z - jnp.max(z, axis=0, keepdims=True))
    lb = jnp.sum(e[:layer + 1], axis=0, keepdims=True) / jnp.sum(e, axis=0, keepdims=True)
    f_gate = lb + (1.0 - lb) * jax.nn.sigmoid(seg(1))
    kk_ref[...] = 1.0 - f_gate
    lg_ref[...] = jnp.log(f_gate)
    ih_ref[...] = seg(2)
    gt_ref[...] = _silu(seg(3))
    fqb_ref[...] = (seg(4) * fox_scale).astype(BF16)
    fk = seg(5)
    fk_ref[...] = fk
    fkb_ref[...] = fk.astype(BF16)
    fv = seg(6)
    fv_ref[...] = fv
    fvb_ref[...] = fv.astype(BF16)
    ff = jnp.dot(xb, wff_ref[...], preferred_element_type=F32) + fb_ref[...]
    lf = jax.nn.log_sigmoid(ff)
    lf_ref[...] = lf[:, :fox_heads]
    lft_ref[...] = lf.T[:fox_heads, :]


def _ab_in(x, w_main, w_ff, lb_logits, f_bias, layer, hw, fw, fox_heads, fox_scale):
    n, d = x.shape
    tm = min(ROW_TILE, n)
    f32_out = lambda w: jax.ShapeDtypeStruct((n, w), F32)
    bf_out = lambda w: jax.ShapeDtypeStruct((n, w), BF16)
    out_shape = ([f32_out(hw)] * 5 + [bf_out(fw)] * 3 + [f32_out(fw)] * 2
                 + [f32_out(fox_heads), jax.ShapeDtypeStruct((fox_heads, n), F32)])
    out_specs = ([_rows(tm, hw)] * 5 + [_rows(tm, fw)] * 5
                 + [_rows(tm, fox_heads), pl.BlockSpec((fox_heads, tm), lambda i: (0, i))])
    return pl.pallas_call(
        functools.partial(_ab_in_kernel, layer=layer, hw=hw, fw=fw, fox_heads=fox_heads,
                          fox_scale=fox_scale),
        grid=(n // tm,),
        in_specs=[_rows(tm, d), _resident(w_main.shape), _resident(w_ff.shape),
                  _resident(lb_logits.shape), _resident(f_bias.shape)],
        out_specs=out_specs,
        out_shape=out_shape,
        compiler_params=_cparams(("parallel",)),
        name="ab_in",
    )(x, w_main, w_ff, lb_logits, f_bias)


def _hgrn_levels(chunk):
    levels, h = [], HG_SUB
    while h < chunk:
        levels.append(h)
        h *= 2
    return levels


def _hgrn_prefix_matrix(chunk):
    t = lax.broadcasted_iota(jnp.int32, (chunk, chunk), 0)
    s = lax.broadcasted_iota(jnp.int32, (chunk, chunk), 1)
    parts = [s <= t]
    for h in _hgrn_levels(chunk):
        ref_row = (t // (2 * h)) * (2 * h) + h - 1
        parts.append(s <= ref_row)
    return jnp.concatenate(parts, axis=0).astype(F32)


def _hgrn_diag(q, k, b, v):
    rows, dk = q.shape
    nb = rows // HG_SUB
    q3, k3, b3 = (a.reshape(nb, HG_SUB, dk) for a in (q, k, b))
    v3 = v.reshape(nb, HG_SUB, v.shape[-1])
    t_loc = lax.broadcasted_iota(jnp.int32, (1, HG_SUB, 1), 1)
    o3 = jnp.zeros(v3.shape, F32)
    for s in range(HG_SUB):
        decay = jnp.exp(jnp.where(t_loc >= s, b3 - b3[:, s:s + 1, :], -jnp.inf))
        a = jnp.sum(q3 * decay * k3[:, s:s + 1, :], axis=-1, keepdims=True)
        o3 = o3 + a * v3[:, s:s + 1, :]
    return o3.reshape(rows, v.shape[-1])


def _hgrn_kernel(*refs, chunk, heads, dk, dv, has_s0):
    if has_s0:
        q_ref, k_ref, g_ref, v_ref, gt_ref, ng_ref, s0_ref, o_ref, st_ref, s_scr = refs
    else:
        q_ref, k_ref, g_ref, v_ref, gt_ref, ng_ref, o_ref, st_ref, s_scr = refs
    tb = pl.program_id(1)

    @pl.when(tb == 0)
    def _():
        s_scr[...] = s0_ref[...] if has_s0 else jnp.zeros(s_scr.shape, F32)

    levels = _hgrn_levels(chunk)
    prefix = _hgrn_prefix_matrix(chunk)
    t = lax.broadcasted_iota(jnp.int32, (chunk, chunk), 0)
    s = lax.broadcasted_iota(jnp.int32, (chunk, chunk), 1)
    masks = [(t // (2 * h) == s // (2 * h)) & (t % (2 * h) >= h) & (s % (2 * h) < h) for h in levels]
    ng = ng_ref[...]

    for c in range(q_ref.shape[0] // chunk):
        rows = slice(c * chunk, (c + 1) * chunk)
        q, k, v = q_ref[rows, :], k_ref[rows, :], v_ref[rows, :]
        stacked = jnp.dot(prefix, g_ref[rows, :], precision=HIGHEST, preferred_element_type=F32)
        b = stacked[:chunk]
        b_last = b[chunk - 1:chunk, :]
        q_in = (q * jnp.exp(b)).astype(BF16)
        k_out = (k * jnp.exp(b_last - b)).astype(BF16)
        vb = v.astype(BF16)
        q_lv, k_lv = [], []
        for j in range(len(levels)):
            w = jnp.exp(-jnp.abs(b - stacked[(j + 1) * chunk:(j + 2) * chunk]))
            q_lv.append((q * w).astype(BF16))
            k_lv.append((k * w).astype(BF16))
        for hd in range(heads):
            ks = slice(hd * dk, (hd + 1) * dk)
            vs = slice(hd * dv, (hd + 1) * dv)
            state = s_scr[hd]
            o = jnp.dot(q_in[:, ks], state.astype(BF16), preferred_element_type=F32)
            o = o + _hgrn_diag(q[:, ks], k[:, ks], b[:, ks], v[:, vs])
            if levels:
                attn = jnp.zeros((chunk, chunk), F32)
                for j in range(len(levels)):
                    blk = lax.dot_general(q_lv[j][:, ks], k_lv[j][:, ks], _NT, preferred_element_type=F32)
                    attn = attn + jnp.where(masks[j], blk, 0.0)
                o = o + jnp.dot(attn.astype(BF16), vb[:, vs], preferred_element_type=F32)
            decay_col = jnp.broadcast_to(jnp.exp(b_last[:, ks]), (dv, dk)).T
            s_scr[hd] = decay_col * state + lax.dot_general(k_out[:, ks], vb[:, vs], _TN,
                                                            preferred_element_type=F32)
            o = o * lax.rsqrt(jnp.mean(o * o, axis=-1, keepdims=True) + NORM_EPS) * ng * gt_ref[rows, vs]
            o_ref[rows, vs] = o.astype(o_ref.dtype)

    @pl.when(tb == pl.num_programs(1) - 1)
    def _():
        st_ref[...] = s_scr[...]


def _hgrn(q, k, g, v, gate, norm_g, s0, *, chunk, rows_per_step, heads, dk, dv):
    nb, t, _ = q.shape
    tb = min(rows_per_step, t)
    seq = lambda w: pl.BlockSpec((None, tb, w), lambda b, i: (b, i, 0))
    st_spec = pl.BlockSpec((None, heads, dk, dv), lambda b, i: (b, 0, 0, 0))
    in_specs = [seq(heads * dk)] * 3 + [seq(heads * dv)] * 2 + [_resident(norm_g.shape)]
    args = [q, k, g, v, gate, norm_g]
    if s0 is not None:
        in_specs.append(st_spec)
        args.append(s0)
    return pl.pallas_call(
        functools.partial(_hgrn_kernel, chunk=chunk, heads=heads, dk=dk, dv=dv, has_s0=s0 is not None),
        grid=(nb, t // tb),
        in_specs=in_specs,
        out_specs=[seq(heads * dv), st_spec],
        out_shape=[jax.ShapeDtypeStruct((nb, t, heads * dv), BF16),
                   jax.ShapeDtypeStruct((nb, heads, dk, dv), F32)],
        scratch_shapes=[pltpu.VMEM((heads, dk, dv), F32)],
        compiler_params=_cparams(("parallel", "arbitrary")),
        name="hgrn_state" if s0 is not None else "hgrn_fresh",
    )(*args)


def _lf_cumsum_kernel(lft_ref, o_ref, x_scr, *, blocks_per_seq):
    heads, n = lft_ref.shape
    nblk = n // LANES
    for i in range(nblk):
        x_scr[i * heads:(i + 1) * heads, :] = lft_ref[:, i * LANES:(i + 1) * LANES]
    r = lax.broadcasted_iota(jnp.int32, (LANES, LANES), 0)
    c = lax.broadcasted_iota(jnp.int32, (LANES, LANES), 1)
    local = jnp.dot(x_scr[...], (r <= c).astype(F32), precision=HIGHEST, preferred_element_type=F32)
    rows = nblk * heads
    ri = lax.broadcasted_iota(jnp.int32, (rows, rows), 0)
    ci = lax.broadcasted_iota(jnp.int32, (rows, rows), 1)
    earlier = ((ri % heads == ci % heads) & (ci // heads < ri // heads)
               & (ci // (heads * blocks_per_seq) == ri // (heads * blocks_per_seq)))
    carry = jnp.dot(earlier.astype(F32), local, precision=HIGHEST, preferred_element_type=F32)
    x_scr[...] = local + carry[:, LANES - 1:LANES]
    for i in range(nblk):
        o_ref[:, i * LANES:(i + 1) * LANES] = x_scr[i * heads:(i + 1) * heads, :]


def _lf_cumsum(lft, seq_len):
    heads, n = lft.shape
    return pl.pallas_call(
        functools.partial(_lf_cumsum_kernel, blocks_per_seq=seq_len // LANES),
        out_shape=jax.ShapeDtypeStruct((heads, n), F32),
        scratch_shapes=[pltpu.VMEM((n // LANES * heads, LANES), F32)],
        compiler_params=pltpu.CompilerParams(vmem_limit_bytes=VMEM_LIMIT_BYTES),
        name="lf_cumsum",
    )(lft)


def _fox_prompt_kernel(qi_ref, ki_ref, q_ref, k_ref, v_ref, fk_ref, fq_ref, o_ref,
                       qm_scr, m_scr, l_scr, acc_scr, *, bq, bk, hd):
    step = pl.program_id(2)
    qi = qi_ref[step]
    ki = ki_ref[step]
    lane = lax.broadcasted_iota(jnp.int32, (1, 2 * hd), 1)

    @pl.when(ki == 0)
    def _():
        q = q_ref[...]
        for e in range(2):
            qm_scr[e] = jnp.where(lane // hd == e, q, jnp.zeros_like(q))
        m_scr[...] = jnp.full(m_scr.shape, -jnp.inf, F32)
        l_scr[...] = jnp.zeros(l_scr.shape, F32)
        acc_scr[...] = jnp.zeros(acc_scr.shape, F32)

    def update(masked):
        kb = k_ref[...]
        vb = v_ref[...]
        if masked:
            row = lax.broadcasted_iota(jnp.int32, (bq, bk), 0)
            col = lax.broadcasted_iota(jnp.int32, (bq, bk), 1)
            visible = (col - row) <= (qi * bq - ki * bk)
        for e in range(2):
            bias = fq_ref[e:e + 1, 0:1] - fk_ref[e:e + 1, :]
            s = lax.dot_general(qm_scr[e], kb, _NT, preferred_element_type=F32) + bias
            if masked:
                s = jnp.where(visible, s, -jnp.inf)
            m_prev = m_scr[e]
            m_new = jnp.maximum(m_prev, jnp.max(s, axis=-1, keepdims=True))
            alpha = jnp.exp(m_prev - m_new)
            p = jnp.exp(s - m_new)
            l_scr[e] = alpha * l_scr[e] + jnp.sum(p, axis=-1, keepdims=True)
            acc_scr[e] = alpha * acc_scr[e] + jnp.dot(p.astype(BF16), vb, preferred_element_type=F32)
            m_scr[e] = m_new

    overlaps_diagonal = (ki + 1) * bk - 1 > qi * bq

    @pl.when(overlaps_diagonal)
    def _():
        update(True)

    @pl.when(jnp.logical_not(overlaps_diagonal))
    def _():
        update(False)

    @pl.when(ki == ((qi + 1) * bq - 1) // bk)
    def _():
        o0 = acc_scr[0] / l_scr[0]
        o1 = acc_scr[1] / l_scr[1]
        o_ref[...] = jnp.where(lane // hd == 0, o0, o1).astype(o_ref.dtype)


def _fox_prompt(q, k, v, fc, *, heads, hd):
    nb, t, w = q.shape
    bq, bk = min(FOX_BQ, t), min(FOX_BK, t)
    pairs = [(i, j) for i in range(t // bq) for j in range(((i + 1) * bq - 1) // bk + 1)]
    qi_tab = jnp.asarray([p[0] for p in pairs], jnp.int32)
    ki_tab = jnp.asarray([p[1] for p in pairs], jnp.int32)
    nqb, nkb = t // bq, t // bk
    fc_pairs = fc.reshape(heads // 2, 2, nb * t)
    grid_spec = pltpu.PrefetchScalarGridSpec(
        num_scalar_prefetch=2,
        grid=(nb, heads // 2, len(pairs)),
        in_specs=[
            pl.BlockSpec((None, bq, 2 * hd), lambda b, hp, s, qi, ki: (b, qi[s], hp)),
            pl.BlockSpec((None, bk, 2 * hd), lambda b, hp, s, qi, ki: (b, ki[s], hp)),
            pl.BlockSpec((None, bk, 2 * hd), lambda b, hp, s, qi, ki: (b, ki[s], hp)),
            pl.BlockSpec((None, 2, bk), lambda b, hp, s, qi, ki: (hp, 0, b * nkb + ki[s])),
            pl.BlockSpec((None, 2, bq), lambda b, hp, s, qi, ki: (hp, 0, b * nqb + qi[s])),
        ],
        out_specs=pl.BlockSpec((None, bq, 2 * hd), lambda b, hp, s, qi, ki: (b, qi[s], hp)),
        scratch_shapes=[pltpu.VMEM((2, bq, 2 * hd), BF16), pltpu.VMEM((2, bq, 1), F32),
                        pltpu.VMEM((2, bq, 1), F32), pltpu.VMEM((2, bq, 2 * hd), F32)],
    )
    return pl.pallas_call(
        functools.partial(_fox_prompt_kernel, bq=bq, bk=bk, hd=hd),
        grid_spec=grid_spec,
        out_shape=jax.ShapeDtypeStruct((nb, t, w), BF16),
        compiler_params=_cparams(("parallel", "parallel", "arbitrary")),
        name="fox_prompt",
    )(qi_tab, ki_tab, q, k, v, fc_pairs, fc_pairs)


def _fox_sample_kernel(*refs, n_pages, heads, hd, page):
    pt_ref, qbd_ref, kn_ref, vn_ref, lfn_ref = refs[:5]
    kc_refs = refs[5:5 + n_pages]
    vc_refs = refs[5 + n_pages:5 + 2 * n_pages]
    lc_refs = refs[5 + 2 * n_pages:5 + 3 * n_pages]
    o_ref, kpad_scr, vpad_scr, lpad_scr, s_scr = refs[5 + 3 * n_pages:]
    del pt_ref
    tq = kn_ref.shape[0]

    kpad_scr[...] = jnp.zeros(kpad_scr.shape, F32)
    vpad_scr[...] = jnp.zeros(vpad_scr.shape, F32)
    lpad_scr[...] = jnp.zeros(lpad_scr.shape, F32)
    kpad_scr[0:tq, :] = kn_ref[...]
    vpad_scr[0:tq, :] = vn_ref[...]
    lpad_scr[0:tq, :] = lfn_ref[...]

    expand = (lax.broadcasted_iota(jnp.int32, (heads, LANES), 1) // tq
              == lax.broadcasted_iota(jnp.int32, (heads, LANES), 0)).astype(BF16)
    lower = (lax.broadcasted_iota(jnp.int32, (page, page), 1)
             <= lax.broadcasted_iota(jnp.int32, (page, page), 0)).astype(BF16)
    carry = jnp.zeros((1, LANES), F32)
    f_blocks = []
    for lf in [r[...] for r in lc_refs] + [lpad_scr[...]]:
        cum = carry
        for part in _split_bf16(lf):
            cols = jnp.dot(part, expand, preferred_element_type=F32).astype(BF16)
            cum = cum + jnp.dot(lower, cols, preferred_element_type=F32)
        carry = cum[page - 1:page, :]
        f_blocks.append(cum)

    def head_rows(page_refs, pad_scr, h, pages):
        pieces = [page_refs[j][pl.ds(h, page, stride=heads), :] if j < n_pages
                  else pad_scr[:, h * hd:(h + 1) * hd] for j in pages]
        return jnp.concatenate(pieces, axis=0).astype(BF16)

    key_new = lax.broadcasted_iota(jnp.int32, (page, LANES), 0)
    query = lax.broadcasted_iota(jnp.int32, (page, LANES), 1) % tq
    col_max = jnp.full((1, LANES), -jnp.inf, F32)
    for first in range(0, n_pages + 1, FOX_SAMPLE_CHUNK_PAGES):
        pages = list(range(first, min(first + FOX_SAMPLE_CHUNK_PAGES, n_pages + 1)))
        s_c = None
        for h in range(heads):
            part = jnp.dot(head_rows(kc_refs, kpad_scr, h, pages), qbd_ref[h], preferred_element_type=F32)
            s_c = part if s_c is None else s_c + part
        s_c = s_c - jnp.concatenate([f_blocks[j] for j in pages], axis=0)
        if pages[-1] == n_pages:
            keep = len(pages) * page - page
            s_new = jnp.where((key_new < tq) & (key_new <= query), s_c[keep:], -jnp.inf)
            s_c = s_new if keep == 0 else jnp.concatenate([s_c[:keep], s_new], axis=0)
        s_scr[first * page:(pages[-1] + 1) * page, :] = s_c
        col_max = jnp.maximum(col_max, jnp.max(s_c, axis=0, keepdims=True))
    p = jnp.exp(s_scr[...] - col_max).T
    denom = jnp.sum(p, axis=1, keepdims=True)
    outs = []
    all_pages = list(range(n_pages + 1))
    for h in range(heads):
        rows = slice(h * tq, (h + 1) * tq)
        o_h = jnp.dot(p[rows].astype(BF16), head_rows(vc_refs, vpad_scr, h, all_pages),
                      preferred_element_type=F32)
        outs.append(o_h / denom[rows])
    o_ref[...] = jnp.concatenate(outs, axis=1)


def _fox_sample(q, k_new, v_new, lf_new, cache_k, cache_v, cache_lf, page_table, *, heads, hd):
    nb, tq, w = q.shape
    n_pages = page_table.shape[1]
    page = cache_k.shape[1]
    assert heads * tq <= LANES
    q_t = q.reshape(nb, tq, heads, hd).transpose(0, 2, 3, 1)
    q_bd = q_t[:, :, :, None, :] * jnp.eye(heads, dtype=q.dtype)[None, :, None, :, None]
    q_bd = jnp.pad(q_bd.reshape(nb, heads, hd, heads * tq), ((0, 0), (0, 0), (0, 0), (0, LANES - heads * tq)))
    cache_k = cache_k.reshape(-1, page * heads, hd)
    cache_v = cache_v.reshape(-1, page * heads, hd)
    new_spec = lambda width: pl.BlockSpec((None, tq, width), lambda b, pt: (b, 0, 0))
    kv_spec = lambda j: pl.BlockSpec((None, page * heads, hd), lambda b, pt: (pt[b * n_pages + j], 0, 0))
    lf_spec = lambda j: pl.BlockSpec((None, page, heads), lambda b, pt: (pt[b * n_pages + j], 0, 0))
    grid_spec = pltpu.PrefetchScalarGridSpec(
        num_scalar_prefetch=1,
        grid=(nb,),
        in_specs=([pl.BlockSpec((None, heads, hd, LANES), lambda b, pt: (b, 0, 0, 0))]
                  + [new_spec(w)] * 2 + [new_spec(heads)]
                  + [kv_spec(j) for j in range(n_pages)] * 2 + [lf_spec(j) for j in range(n_pages)]),
        out_specs=new_spec(w),
        scratch_shapes=[pltpu.VMEM((page, w), F32), pltpu.VMEM((page, w), F32), pltpu.VMEM((page, heads), F32),
                        pltpu.VMEM(((n_pages + 1) * page, LANES), F32)],
    )
    return pl.pallas_call(
        functools.partial(_fox_sample_kernel, n_pages=n_pages, heads=heads, hd=hd, page=page),
        grid_spec=grid_spec,
        out_shape=jax.ShapeDtypeStruct((nb, tq, w), F32),
        compiler_params=_cparams(("parallel",)),
        name="fox_sample",
    )(page_table.reshape(-1), q_bd, k_new, v_new, lf_new,
      *([cache_k] * n_pages), *([cache_v] * n_pages), *([cache_lf] * n_pages))


def _ab_out_ln_kernel(x_ref, oh_ref, of_ref, wh_ref, wf_ref, g_ref, b_ref, o_ref, *, alpha):
    y = (jnp.dot(oh_ref[...].astype(BF16), wh_ref[...], preferred_element_type=F32)
         + jnp.dot(of_ref[...].astype(BF16), wf_ref[...], preferred_element_type=F32))
    o_ref[...] = _layer_norm(alpha * x_ref[...] + y, g_ref[...], b_ref[...])


def _ab_out_ln(x, oh, of, wh, wf, g, b, alpha):
    n, d = x.shape
    tm = min(ROW_TILE, n)
    return pl.pallas_call(
        functools.partial(_ab_out_ln_kernel, alpha=alpha),
        grid=(n // tm,),
        in_specs=[_rows(tm, d), _rows(tm, oh.shape[1]), _rows(tm, of.shape[1]), _resident(wh.shape),
                  _resident(wf.shape), _resident((1, d)), _resident((1, d))],
        out_specs=_rows(tm, d),
        out_shape=jax.ShapeDtypeStruct((n, d), F32),
        compiler_params=_cparams(("parallel",)),
        name="ab_out_ln",
    )(x, oh, of, wh, wf, g, b)


def _gmlp_in_kernel(x_ref, w_ref, g_ref, b_ref, u_ref, v_ref):
    xb = x_ref[...].astype(BF16)
    half = u_ref.shape[1]
    u_ref[...] = jax.nn.gelu(jnp.dot(xb, w_ref[:, :half], preferred_element_type=F32))
    v = jax.nn.gelu(jnp.dot(xb, w_ref[:, half:], preferred_element_type=F32))
    v_ref[...] = _layer_norm(v, g_ref[...], b_ref[...])


def _gmlp_in(x, w, g, b):
    n, d = x.shape
    half = w.shape[1] // 2
    tm = min(ROW_TILE, n)
    return pl.pallas_call(
        _gmlp_in_kernel,
        grid=(n // tm,),
        in_specs=[_rows(tm, d), _resident(w.shape), _resident((1, half)), _resident((1, half))],
        out_specs=[_rows(tm, half), _rows(tm, half)],
        out_shape=[jax.ShapeDtypeStruct((n, half), F32)] * 2,
        compiler_params=_cparams(("parallel",)),
        name="gmlp_in",
    )(x, w, g, b)


def _gmlp_mix_out_ln_kernel(x_ref, u_ref, v_ref, ws_ref, bs_ref, wo_ref, g_ref, b_ref, o_ref, z_scr,
                            *, alpha, period):
    groups, ch, _ = ws_ref.shape
    gw = v_ref.shape[1] // groups
    t = lax.broadcasted_iota(jnp.int32, (ch, ch), 0)
    s = lax.broadcasted_iota(jnp.int32, (ch, ch), 1)
    causal = (t // period == s // period) & (s % period <= t % period)
    for g in range(groups):
        w_g = jnp.where(causal, ws_ref[g], 0.0).astype(BF16)
        cols = slice(g * gw, (g + 1) * gw)
        for c in range(x_ref.shape[0] // ch):
            rows = slice(c * ch, (c + 1) * ch)
            mixed = (jnp.dot(w_g, v_ref[rows, cols].astype(BF16), preferred_element_type=F32)
                     + bs_ref[:, g:g + 1])
            z_scr[rows, cols] = (u_ref[rows, cols] * mixed).astype(BF16)
    y = jnp.dot(z_scr[...], wo_ref[...], preferred_element_type=F32)
    o_ref[...] = _layer_norm(alpha * x_ref[...] + y, g_ref[...], b_ref[...])


def _gmlp_mix_out_ln(x, u, v, w_s, b_s_t, w_out, g, b, alpha, period):
    n, d = x.shape
    half = u.shape[1]
    tm = min(ROW_TILE, n)
    return pl.pallas_call(
        functools.partial(_gmlp_mix_out_ln_kernel, alpha=alpha, period=period),
        grid=(n // tm,),
        in_specs=[_rows(tm, d), _rows(tm, half), _rows(tm, half), _resident(w_s.shape),
                  _resident(b_s_t.shape), _resident(w_out.shape), _resident((1, d)), _resident((1, d))],
        out_specs=_rows(tm, d),
        out_shape=jax.ShapeDtypeStruct((n, d), F32),
        scratch_shapes=[pltpu.VMEM((tm, half), BF16)],
        compiler_params=_cparams(("parallel",)),
        name="gmlp_mix_out_ln",
    )(x, u, v, w_s, b_s_t, w_out, g, b)


def kernel(x_prompt, x_sample, cache_fox_k, cache_fox_v, cache_fox_logf, state_hg, page_table, ln_g, ln_b,
           ffn_w_gate, ffn_w_up, ffn_w_down, ab_w_in, hg_lb_logits, hg_norm_g, fox_f_bias, ab_w_out, c_w_in,
           c_ln_g, c_ln_b, c_w_s, c_b_s, c_w_out):
    nb_p, t_p, d = x_prompt.shape
    nb_s, t_s, _ = x_sample.shape
    depth = ln_g.shape[0]
    alpha = (2.0 * depth) ** 0.25
    hg_heads, hg_k, hg_v = state_hg.shape[2:]
    fox_heads, fox_hd = cache_fox_k.shape[3:]
    hw, fw = hg_heads * hg_k, fox_heads * fox_hd
    assert hg_k == hg_v and hw == hg_heads * hg_v

    xp = x_prompt.reshape(nb_p * t_p, d)
    xs = x_sample.reshape(nb_s * t_s, d)
    row = lambda a: a.reshape(1, -1)
    outs = {name: [] for name in ("k_p", "v_p", "lf_p", "hg_p", "k_s", "v_s", "lf_s", "hg_s", "cv_s")}

    for l in range(depth):
        j = l // 2

        def half_step(x, i):
            return _ffn_ln(x, ffn_w_gate[l, i].astype(BF16), ffn_w_up[l, i].astype(BF16),
                           ffn_w_down[l, i].astype(BF16), row(ln_g[l, i]), row(ln_b[l, i]), alpha)

        xp, xs = half_step(xp, 0), half_step(xs, 0)
        if l % 2 == 0:
            main_w = 4 * hw + 3 * fw
            w_main = ab_w_in[j, :, :main_w].astype(BF16)
            w_ff = jnp.pad(ab_w_in[j, :, main_w:], ((0, 0), (0, LANES - fox_heads))).astype(BF16)
            f_bias = jnp.pad(fox_f_bias[j], (0, LANES - fox_heads)).reshape(1, LANES)
            wh = ab_w_out[j, :hw].astype(BF16)
            wf = ab_w_out[j, hw:].astype(BF16)
            norm_g = row(hg_norm_g[j])

            def project(x):
                return _ab_in(x, w_main, w_ff, hg_lb_logits, f_bias, l, hw, fw, fox_heads, fox_hd ** -0.5)

            qh, kk, lg, ih, gt, fqb, fkb, fvb, fk, fv, lf, lft = project(xp)
            seq = lambda a: a.reshape(nb_p, t_p, a.shape[-1])
            o_h, s_t = _hgrn(seq(qh), seq(kk), seq(lg), seq(ih), seq(gt), norm_g, None, chunk=HG_CHUNK,
                             rows_per_step=HG_ROWS_PER_STEP, heads=hg_heads, dk=hg_k, dv=hg_v)
            fc = _lf_cumsum(lft, t_p)
            o_f = _fox_prompt(seq(fqb), seq(fkb), seq(fvb), fc, heads=fox_heads, hd=fox_hd)
            xp = _ab_out_ln(xp, o_h.reshape(-1, hw), o_f.reshape(-1, fw), wh, wf,
                            row(ln_g[l, 1]), row(ln_b[l, 1]), alpha)
            outs["k_p"].append(fk.reshape(nb_p, t_p, fox_heads, fox_hd))
            outs["v_p"].append(fv.reshape(nb_p, t_p, fox_heads, fox_hd))
            outs["lf_p"].append(lf.reshape(nb_p, t_p, fox_heads))
            outs["hg_p"].append(s_t)

            qh, kk, lg, ih, gt, fqb, fkb, fvb, fk, fv, lf, lft = project(xs)
            seq = lambda a: a.reshape(nb_s, t_s, a.shape[-1])
            o_h, s_t = _hgrn(seq(qh), seq(kk), seq(lg), seq(ih), seq(gt), norm_g, state_hg[j], chunk=t_s,
                             rows_per_step=t_s, heads=hg_heads, dk=hg_k, dv=hg_v)
            o_f = _fox_sample(seq(fqb), seq(fk), seq(fv), seq(lf), cache_fox_k[j], cache_fox_v[j],
                              cache_fox_logf[j], page_table, heads=fox_heads, hd=fox_hd)
            xs = _ab_out_ln(xs, o_h.reshape(-1, hw), o_f.reshape(-1, fw), wh, wf,
                            row(ln_g[l, 1]), row(ln_b[l, 1]), alpha)
            outs["k_s"].append(fk.reshape(nb_s, t_s, fox_heads, fox_hd))
            outs["v_s"].append(fv.reshape(nb_s, t_s, fox_heads, fox_hd))
            outs["lf_s"].append(lf.reshape(nb_s, t_s, fox_heads))
            outs["hg_s"].append(s_t)
        else:
            w_in = c_w_in[j].astype(BF16)
            w_out = c_w_out[j].astype(BF16)
            reps = C_CHUNK // t_s
            w_s_sample = jnp.tile(c_w_s[j][:, :t_s, :t_s], (1, reps, reps))
            b_s_sample = jnp.tile(c_b_s[j][:, :t_s], (1, reps))

            def mixer(x, w_s, b_s, period):
                u, v = _gmlp_in(x, w_in, row(c_ln_g[j]), row(c_ln_b[j]))
                y = _gmlp_mix_out_ln(x, u, v, w_s, b_s.T, w_out, row(ln_g[l, 1]), row(ln_b[l, 1]),
                                     alpha, period)
                return y, v

            xp, _ = mixer(xp, c_w_s[j], c_b_s[j], C_CHUNK)
            xs, v_rows = mixer(xs, w_s_sample, b_s_sample, t_s)
            outs["cv_s"].append(v_rows.reshape(nb_s, t_s, -1))
        xp, xs = half_step(xp, 2), half_step(xs, 2)

    stack = lambda name: jnp.stack(outs[name])
    return (xp.reshape(nb_p, t_p, d), xs.reshape(nb_s, t_s, d),
            stack("k_p"), stack("v_p"), stack("lf_p"), stack("hg_p"),
            stack("k_s"), stack("v_s"), stack("lf_s"), stack("hg_s"), stack("cv_s"))
```

```python
import functools

import jax
import jax.numpy as jnp
import numpy as np
from jax import lax
from jax.experimental import pallas as pl
from jax.experimental.pallas import tpu as pltpu

F32 = jnp.float32
BF16 = jnp.bfloat16
HIGHEST = lax.Precision.HIGHEST

NORM_EPS = 1e-5
LANES = 128
SUBLANES = 8
VMEM_LIMIT_BYTES = 56 * 1024 * 1024

ROW_TILE = 512
FFN_COL_CHUNK = 256
HG_SUB = SUBLANES
HG_CHUNK = 64
HG_ROWS_PER_STEP = 256
FOX_BQ = 1024
FOX_BK = 1024
FOX_SAMPLE_CHUNK_PAGES = 2
C_CHUNK = 128

_NT = (((1,), (1,)), ((), ()))
_TN = (((0,), (0,)), ((), ()))


def _cparams(semantics):
    return pltpu.CompilerParams(dimension_semantics=semantics, vmem_limit_bytes=VMEM_LIMIT_BYTES)


def _resident(shape):
    nd = len(shape)
    return pl.BlockSpec(shape, lambda *_: (0,) * nd, pipeline_mode=pl.Buffered(1))


def _rows(tm, width):
    return pl.BlockSpec((tm, width), lambda i: (i, 0))


def _layer_norm(y, g, b):
    mu = jnp.mean(y, axis=-1, keepdims=True)
    d = y - mu
    var = jnp.mean(d * d, axis=-1, keepdims=True)
    return d * lax.rsqrt(var + NORM_EPS) * g + b


def _silu(x):
    return x * jax.nn.sigmoid(x)


def _split_bf16(x, pieces=3):
    out = []
    for _ in range(pieces):
        piece = x.astype(BF16)
        out.append(piece)
        x = x - piece.astype(F32)
    return out


def _ffn_ln_kernel(x_ref, wg_ref, wu_ref, wd_ref, g_ref, b_ref, o_ref, a_scr, *, alpha):
    x = x_ref[...]
    xb = x.astype(BF16)
    hidden = wg_ref.shape[1]
    for c in range(hidden // FFN_COL_CHUNK):
        sl = slice(c * FFN_COL_CHUNK, (c + 1) * FFN_COL_CHUNK)
        gate = jnp.dot(xb, wg_ref[:, sl], preferred_element_type=F32)
        up = jnp.dot(xb, wu_ref[:, sl], preferred_element_type=F32)
        a_scr[:, sl] = (_silu(gate) * up).astype(BF16)
    y = jnp.dot(a_scr[...], wd_ref[...], preferred_element_type=F32)
    o_ref[...] = _layer_norm(alpha * x + 0.5 * y, g_ref[...], b_ref[...])


def _ffn_ln(x, wg, wu, wd, g, b, alpha):
    n, d = x.shape
    f = wg.shape[1]
    tm = min(ROW_TILE, n)
    return pl.pallas_call(
        functools.partial(_ffn_ln_kernel, alpha=alpha),
        grid=(n // tm,),
        in_specs=[_rows(tm, d), _resident((d, f)), _resident((d, f)), _resident((f, d)),
                  _resident((1, d)), _resident((1, d))],
        out_specs=_rows(tm, d),
        out_shape=jax.ShapeDtypeStruct((n, d), F32),
        scratch_shapes=[pltpu.VMEM((tm, f), BF16)],
        compiler_params=_cparams(("parallel",)),
        name="ffn_ln",
    )(x, wg, wu, wd, g, b)


def _ab_in_kernel(x_ref, w_ref, wff_ref, lbl_ref, fb_ref,
                  qh_ref, kk_ref, lg_ref, ih_ref, gt_ref, fqb_ref, fkb_ref, fvb_ref, fk_ref, fv_ref,
                  lf_ref, lft_ref, *, layer, hw, fw, fox_heads, fox_scale):
    xb = x_ref[...].astype(BF16)
    offs = np.cumsum([0, hw, hw, hw, hw, fw, fw, fw]).tolist()

    def seg(i):
        return jnp.dot(xb, w_ref[:, offs[i]:offs[i + 1]], preferred_element_type=F32)

    qh_ref[...] = _silu(seg(0))
    z = lbl_ref[...]
    e = jnp.exp(z - jnp.max(z, axis=0, keepdims=True))
    lb = jnp.sum(e[:layer + 1], axis=0, keepdims=True) / jnp.sum(e, axis=0, keepdims=True)
    f_gate = lb + (1.0 - lb) * jax.nn.sigmoid(seg(1))
    kk_ref[...] = 1.0 - f_gate
    lg_ref[...] = jnp.log(f_gate)
    ih_ref[...] = seg(2)
    gt_ref[...] = _silu(seg(3))
    fqb_ref[...] = (seg(4) * fox_scale).astype(BF16)
    fk = seg(5)
    fk_ref[...] = fk
    fkb_ref[...] = fk.astype(BF16)
    fv = seg(6)
    fv_ref[...] = fv
    fvb_ref[...] = fv.astype(BF16)
    ff = jnp.dot(xb, wff_ref[...], preferred_element_type=F32) + fb_ref[...]
    lf = jax.nn.log_sigmoid(ff)
    lf_ref[...] = lf[:, :fox_heads]
    lft_ref[...] = lf.T[:fox_heads, :]


def _ab_in(x, w_main, w_ff, lb_logits, f_bias, layer, hw, fw, fox_heads, fox_scale):
    n, d = x.shape
    tm = min(ROW_TILE, n)
    f32_out = lambda w: jax.ShapeDtypeStruct((n, w), F32)
    bf_out = lambda w: jax.ShapeDtypeStruct((n, w), BF16)
    out_shape = ([f32_out(hw)] * 5 + [bf_out(fw)] * 3 + [f32_out(fw)] * 2
                 + [f32_out(fox_heads), jax.ShapeDtypeStruct((fox_heads, n), F32)])
    out_specs = ([_rows(tm, hw)] * 5 + [_rows(tm, fw)] * 5
                 + [_rows(tm, fox_heads), pl.BlockSpec((fox_heads, tm), lambda i: (0, i))])
    return pl.pallas_call(
        functools.partial(_ab_in_kernel, layer=layer, hw=hw, fw=fw, fox_heads=fox_heads,
                          fox_scale=fox_scale),
        grid=(n // tm,),
        in_specs=[_rows(tm, d), _resident(w_main.shape), _resident(w_ff.shape),
                  _resident(lb_logits.shape), _resident(f_bias.shape)],
        out_specs=out_specs,
        out_shape=out_shape,
        compiler_params=_cparams(("parallel",)),
        name="ab_in",
    )(x, w_main, w_ff, lb_logits, f_bias)


def _hgrn_levels(chunk):
    levels, h = [], HG_SUB
    while h < chunk:
        levels.append(h)
        h *= 2
    return levels


def _hgrn_prefix_matrix(chunk):
    t = lax.broadcasted_iota(jnp.int32, (chunk, chunk), 0)
    s = lax.broadcasted_iota(jnp.int32, (chunk, chunk), 1)
    parts = [s <= t]
    for h in _hgrn_levels(chunk):
        ref_row = (t // (2 * h)) * (2 * h) + h - 1
        parts.append(s <= ref_row)
    return jnp.concatenate(parts, axis=0).astype(F32)


def _hgrn_diag(q, k, b, v):
    rows, dk = q.shape
    nb = rows // HG_SUB
    q3, k3, b3 = (a.reshape(nb, HG_SUB, dk) for a in (q, k, b))
    v3 = v.reshape(nb, HG_SUB, v.shape[-1])
    t_loc = lax.broadcasted_iota(jnp.int32, (1, HG_SUB, 1), 1)
    o3 = jnp.zeros(v3.shape, F32)
    for s in range(HG_SUB):
        decay = jnp.exp(jnp.where(t_loc >= s, b3 - b3[:, s:s + 1, :], -jnp.inf))
        a = jnp.sum(q3 * decay * k3[:, s:s + 1, :], axis=-1, keepdims=True)
        o3 = o3 + a * v3[:, s:s + 1, :]
    return o3.reshape(rows, v.shape[-1])


def _hgrn_kernel(*refs, chunk, heads, dk, dv, has_s0):
    if has_s0:
        q_ref, k_ref, g_ref, v_ref, gt_ref, ng_ref, s0_ref, o_ref, st_ref, s_scr = refs
    else:
        q_ref, k_ref, g_ref, v_ref, gt_ref, ng_ref, o_ref, st_ref, s_scr = refs
    tb = pl.program_id(1)

    @pl.when(tb == 0)
    def _():
        s_scr[...] = s0_ref[...] if has_s0 else jnp.zeros(s_scr.shape, F32)

    levels = _hgrn_levels(chunk)
    prefix = _hgrn_prefix_matrix(chunk)
    t = lax.broadcasted_iota(jnp.int32, (chunk, chunk), 0)
    s = lax.broadcasted_iota(jnp.int32, (chunk, chunk), 1)
    masks = [(t // (2 * h) == s // (2 * h)) & (t % (2 * h) >= h) & (s % (2 * h) < h) for h in levels]
    ng = ng_ref[...]

    for c in range(q_ref.shape[0] // chunk):
        rows = slice(c * chunk, (c + 1) * chunk)
        q, k, v = q_ref[rows, :], k_ref[rows, :], v_ref[rows, :]
        stacked = jnp.dot(prefix, g_ref[rows, :], precision=HIGHEST, preferred_element_type=F32)
        b = stacked[:chunk]
        b_last = b[chunk - 1:chunk, :]
        q_in = (q * jnp.exp(b)).astype(BF16)
        k_out = (k * jnp.exp(b_last - b)).astype(BF16)
        vb = v.astype(BF16)
        q_lv, k_lv = [], []
        for j in range(len(levels)):
            w = jnp.exp(-jnp.abs(b - stacked[(j + 1) * chunk:(j + 2) * chunk]))
            q_lv.append((q * w).astype(BF16))
            k_lv.append((k * w).astype(BF16))
        for hd in range(heads):
            ks = slice(hd * dk, (hd + 1) * dk)
            vs = slice(hd * dv, (hd + 1) * dv)
            state = s_scr[hd]
            o = jnp.dot(q_in[:, ks], state.astype(BF16), preferred_element_type=F32)
            o = o + _hgrn_diag(q[:, ks], k[:, ks], b[:, ks], v[:, vs])
            if levels:
                attn = jnp.zeros((chunk, chunk), F32)
                for j in range(len(levels)):
                    blk = lax.dot_general(q_lv[j][:, ks], k_lv[j][:, ks], _NT, preferred_element_type=F32)
                    attn = attn + jnp.where(masks[j], blk, 0.0)
                o = o + jnp.dot(attn.astype(BF16), vb[:, vs], preferred_element_type=F32)
            decay_col = jnp.broadcast_to(jnp.exp(b_last[:, ks]), (dv, dk)).T
            s_scr[hd] = decay_col * state + lax.dot_general(k_out[:, ks], vb[:, vs], _TN,
                                                            preferred_element_type=F32)
            o = o * lax.rsqrt(jnp.mean(o * o, axis=-1, keepdims=True) + NORM_EPS) * ng * gt_ref[rows, vs]
            o_ref[rows, vs] = o.astype(o_ref.dtype)

    @pl.when(tb == pl.num_programs(1) - 1)
    def _():
        st_ref[...] = s_scr[...]


def _hgrn(q, k, g, v, gate, norm_g, s0, *, chunk, rows_per_step, heads, dk, dv):
    nb, t, _ = q.shape
    tb = min(rows_per_step, t)
    seq = lambda w: pl.BlockSpec((None, tb, w), lambda b, i: (b, i, 0))
    st_spec = pl.BlockSpec((None, heads, dk, dv), lambda b, i: (b, 0, 0, 0))
    in_specs = [seq(heads * dk)] * 3 + [seq(heads * dv)] * 2 + [_resident(norm_g.shape)]
    args = [q, k, g, v, gate, norm_g]
    if s0 is not None:
        in_specs.append(st_spec)
        args.append(s0)
    return pl.pallas_call(
        functools.partial(_hgrn_kernel, chunk=chunk, heads=heads, dk=dk, dv=dv, has_s0=s0 is not None),
        grid=(nb, t // tb),
        in_specs=in_specs,
        out_specs=[seq(heads * dv), st_spec],
        out_shape=[jax.ShapeDtypeStruct((nb, t, heads * dv), BF16),
                   jax.ShapeDtypeStruct((nb, heads, dk, dv), F32)],
        scratch_shapes=[pltpu.VMEM((heads, dk, dv), F32)],
        compiler_params=_cparams(("parallel", "arbitrary")),
        name="hgrn_state" if s0 is not None else "hgrn_fresh",
    )(*args)


def _lf_cumsum_kernel(lft_ref, o_ref, x_scr, *, blocks_per_seq):
    heads, n = lft_ref.shape
    nblk = n // LANES
    for i in range(nblk):
        x_scr[i * heads:(i + 1) * heads, :] = lft_ref[:, i * LANES:(i + 1) * LANES]
    r = lax.broadcasted_iota(jnp.int32, (LANES, LANES), 0)
    c = lax.broadcasted_iota(jnp.int32, (LANES, LANES), 1)
    local = jnp.dot(x_scr[...], (r <= c).astype(F32), precision=HIGHEST, preferred_element_type=F32)
    rows = nblk * heads
    ri = lax.broadcasted_iota(jnp.int32, (rows, rows), 0)
    ci = lax.broadcasted_iota(jnp.int32, (rows, rows), 1)
    earlier = ((ri % heads == ci % heads) & (ci // heads < ri // heads)
               & (ci // (heads * blocks_per_seq) == ri // (heads * blocks_per_seq)))
    carry = jnp.dot(earlier.astype(F32), local, precision=HIGHEST, preferred_element_type=F32)
    x_scr[...] = local + carry[:, LANES - 1:LANES]
    for i in range(nblk):
        o_ref[:, i * LANES:(i + 1) * LANES] = x_scr[i * heads:(i + 1) * heads, :]


def _lf_cumsum(lft, seq_len):
    heads, n = lft.shape
    return pl.pallas_call(
        functools.partial(_lf_cumsum_kernel, blocks_per_seq=seq_len // LANES),
        out_shape=jax.ShapeDtypeStruct((heads, n), F32),
        scratch_shapes=[pltpu.VMEM((n // LANES * heads, LANES), F32)],
        compiler_params=pltpu.CompilerParams(vmem_limit_bytes=VMEM_LIMIT_BYTES),
        name="lf_cumsum",
    )(lft)


def _fox_prompt_kernel(qi_ref, ki_ref, q_ref, k_ref, v_ref, fk_ref, fq_ref, o_ref,
                       qm_scr, m_scr, l_scr, acc_scr, *, bq, bk, hd):
    step = pl.program_id(2)
    qi = qi_ref[step]
    ki = ki_ref[step]
    lane = lax.broadcasted_iota(jnp.int32, (1, 2 * hd), 1)

    @pl.when(ki == 0)
    def _():
        q = q_ref[...]
        for e in range(2):
            qm_scr[e] = jnp.where(lane // hd == e, q, jnp.zeros_like(q))
        m_scr[...] = jnp.full(m_scr.shape, -jnp.inf, F32)
        l_scr[...] = jnp.zeros(l_scr.shape, F32)
        acc_scr[...] = jnp.zeros(acc_scr.shape, F32)

    def update(masked):
        kb = k_ref[...]
        vb = v_ref[...]
        if masked:
            row = lax.broadcasted_iota(jnp.int32, (bq, bk), 0)
            col = lax.broadcasted_iota(jnp.int32, (bq, bk), 1)
            visible = (col - row) <= (qi * bq - ki * bk)
        for e in range(2):
            bias = fq_ref[e:e + 1, 0:1] - fk_ref[e:e + 1, :]
            s = lax.dot_general(qm_scr[e], kb, _NT, preferred_element_type=F32) + bias
            if masked:
                s = jnp.where(visible, s, -jnp.inf)
            m_prev = m_scr[e]
            m_new = jnp.maximum(m_prev, jnp.max(s, axis=-1, keepdims=True))
            alpha = jnp.exp(m_prev - m_new)
            p = jnp.exp(s - m_new)
            l_scr[e] = alpha * l_scr[e] + jnp.sum(p, axis=-1, keepdims=True)
            acc_scr[e] = alpha * acc_scr[e] + jnp.dot(p.astype(BF16), vb, preferred_element_type=F32)
            m_scr[e] = m_new

    overlaps_diagonal = (ki + 1) * bk - 1 > qi * bq

    @pl.when(overlaps_diagonal)
    def _():
        update(True)

    @pl.when(jnp.logical_not(overlaps_diagonal))
    def _():
        update(False)

    @pl.when(ki == ((qi + 1) * bq - 1) // bk)
    def _():
        o0 = acc_scr[0] / l_scr[0]
        o1 = acc_scr[1] / l_scr[1]
        o_ref[...] = jnp.where(lane // hd == 0, o0, o1).astype(o_ref.dtype)


def _fox_prompt(q, k, v, fc, *, heads, hd):
    nb, t, w = q.shape
    bq, bk = min(FOX_BQ, t), min(FOX_BK, t)
    pairs = [(i, j) for i in range(t // bq) for j in range(((i + 1) * bq - 1) // bk + 1)]
    qi_tab = jnp.asarray([p[0] for p in pairs], jnp.int32)
    ki_tab = jnp.asarray([p[1] for p in pairs], jnp.int32)
    nqb, nkb = t // bq, t // bk
    fc_pairs = fc.reshape(heads // 2, 2, nb * t)
    grid_spec = pltpu.PrefetchScalarGridSpec(
        num_scalar_prefetch=2,
        grid=(nb, heads // 2, len(pairs)),
        in_specs=[
            pl.BlockSpec((None, bq, 2 * hd), lambda b, hp, s, qi, ki: (b, qi[s], hp)),
            pl.BlockSpec((None, bk, 2 * hd), lambda b, hp, s, qi, ki: (b, ki[s], hp)),
            pl.BlockSpec((None, bk, 2 * hd), lambda b, hp, s, qi, ki: (b, ki[s], hp)),
            pl.BlockSpec((None, 2, bk), lambda b, hp, s, qi, ki: (hp, 0, b * nkb + ki[s])),
            pl.BlockSpec((None, 2, bq), lambda b, hp, s, qi, ki: (hp, 0, b * nqb + qi[s])),
        ],
        out_specs=pl.BlockSpec((None, bq, 2 * hd), lambda b, hp, s, qi, ki: (b, qi[s], hp)),
        scratch_shapes=[pltpu.VMEM((2, bq, 2 * hd), BF16), pltpu.VMEM((2, bq, 1), F32),
                        pltpu.VMEM((2, bq, 1), F32), pltpu.VMEM((2, bq, 2 * hd), F32)],
    )
    return pl.pallas_call(
        functools.partial(_fox_prompt_kernel, bq=bq, bk=bk, hd=hd),
        grid_spec=grid_spec,
        out_shape=jax.ShapeDtypeStruct((nb, t, w), BF16),
        compiler_params=_cparams(("parallel", "parallel", "arbitrary")),
        name="fox_prompt",
    )(qi_tab, ki_tab, q, k, v, fc_pairs, fc_pairs)


def _fox_sample_kernel(*refs, n_pages, heads, hd, page):
    pt_ref, qbd_ref, kn_ref, vn_ref, lfn_ref = refs[:5]
    kc_refs = refs[5:5 + n_pages]
    vc_refs = refs[5 + n_pages:5 + 2 * n_pages]
    lc_refs = refs[5 + 2 * n_pages:5 + 3 * n_pages]
    o_ref, kpad_scr, vpad_scr, lpad_scr, s_scr = refs[5 + 3 * n_pages:]
    del pt_ref
    tq = kn_ref.shape[0]

    kpad_scr[...] = jnp.zeros(kpad_scr.shape, F32)
    vpad_scr[...] = jnp.zeros(vpad_scr.shape, F32)
    lpad_scr[...] = jnp.zeros(lpad_scr.shape, F32)
    kpad_scr[0:tq, :] = kn_ref[...]
    vpad_scr[0:tq, :] = vn_ref[...]
    lpad_scr[0:tq, :] = lfn_ref[...]

    expand = (lax.broadcasted_iota(jnp.int32, (heads, LANES), 1) // tq
              == lax.broadcasted_iota(jnp.int32, (heads, LANES), 0)).astype(BF16)
    lower = (lax.broadcasted_iota(jnp.int32, (page, page), 1)
             <= lax.broadcasted_iota(jnp.int32, (page, page), 0)).astype(BF16)
    carry = jnp.zeros((1, LANES), F32)
    f_blocks = []
    for lf in [r[...] for r in lc_refs] + [lpad_scr[...]]:
        cum = carry
        for part in _split_bf16(lf):
            cols = jnp.dot(part, expand, preferred_element_type=F32).astype(BF16)
            cum = cum + jnp.dot(lower, cols, preferred_element_type=F32)
        carry = cum[page - 1:page, :]
        f_blocks.append(cum)

    def head_rows(page_refs, pad_scr, h, pages):
        pieces = [page_refs[j].reshape(page * heads, hd)[pl.ds(h, page, stride=heads), :] if j < n_pages
                  else pad_scr[:, h * hd:(h + 1) * hd] for j in pages]
        return jnp.concatenate(pieces, axis=0).astype(BF16)

    key_new = lax.broadcasted_iota(jnp.int32, (page, LANES), 0)
    query = lax.broadcasted_iota(jnp.int32, (page, LANES), 1) % tq
    col_max = jnp.full((1, LANES), -jnp.inf, F32)
    for first in range(0, n_pages + 1, FOX_SAMPLE_CHUNK_PAGES):
        pages = list(range(first, min(first + FOX_SAMPLE_CHUNK_PAGES, n_pages + 1)))
        s_c = None
        for h in range(heads):
            part = jnp.dot(head_rows(kc_refs, kpad_scr, h, pages), qbd_ref[h], preferred_element_type=F32)
            s_c = part if s_c is None else s_c + part
        s_c = s_c - jnp.concatenate([f_blocks[j] for j in pages], axis=0)
        if pages[-1] == n_pages:
            keep = len(pages) * page - page
            s_new = jnp.where((key_new < tq) & (key_new <= query), s_c[keep:], -jnp.inf)
            s_c = s_new if keep == 0 else jnp.concatenate([s_c[:keep], s_new], axis=0)
        s_scr[first * page:(pages[-1] + 1) * page, :] = s_c
        col_max = jnp.maximum(col_max, jnp.max(s_c, axis=0, keepdims=True))
    p = jnp.exp(s_scr[...] - col_max).T
    denom = jnp.sum(p, axis=1, keepdims=True)
    outs = []
    all_pages = list(range(n_pages + 1))
    for h in range(heads):
        rows = slice(h * tq, (h + 1) * tq)
        o_h = jnp.dot(p[rows].astype(BF16), head_rows(vc_refs, vpad_scr, h, all_pages),
                      preferred_element_type=F32)
        outs.append(o_h / denom[rows])
    o_ref[...] = jnp.concatenate(outs, axis=1)


def _fox_sample(q, k_new, v_new, lf_new, cache_k, cache_v, cache_lf, page_table, *, heads, hd):
    nb, tq, w = q.shape
    n_pages = page_table.shape[1]
    page = cache_k.shape[1]
    assert heads * tq <= LANES
    q_t = q.reshape(nb, tq, heads, hd).transpose(0, 2, 3, 1)
    q_bd = q_t[:, :, :, None, :] * jnp.eye(heads, dtype=q.dtype)[None, :, None, :, None]
    q_bd = jnp.pad(q_bd.reshape(nb, heads, hd, heads * tq), ((0, 0), (0, 0), (0, 0), (0, LANES - heads * tq)))
    new_spec = lambda width: pl.BlockSpec((None, tq, width), lambda b, pt: (b, 0, 0))
    kv_spec = lambda j: pl.BlockSpec((None, page, heads, hd), lambda b, pt: (pt[b * n_pages + j], 0, 0, 0))
    lf_spec = lambda j: pl.BlockSpec((None, page, heads), lambda b, pt: (pt[b * n_pages + j], 0, 0))
    grid_spec = pltpu.PrefetchScalarGridSpec(
        num_scalar_prefetch=1,
        grid=(nb,),
        in_specs=([pl.BlockSpec((None, heads, hd, LANES), lambda b, pt: (b, 0, 0, 0))]
                  + [new_spec(w)] * 2 + [new_spec(heads)]
                  + [kv_spec(j) for j in range(n_pages)] * 2 + [lf_spec(j) for j in range(n_pages)]),
        out_specs=new_spec(w),
        scratch_shapes=[pltpu.VMEM((page, w), F32), pltpu.VMEM((page, w), F32), pltpu.VMEM((page, heads), F32),
                        pltpu.VMEM(((n_pages + 1) * page, LANES), F32)],
    )
    return pl.pallas_call(
        functools.partial(_fox_sample_kernel, n_pages=n_pages, heads=heads, hd=hd, page=page),
        grid_spec=grid_spec,
        out_shape=jax.ShapeDtypeStruct((nb, tq, w), F32),
        compiler_params=_cparams(("parallel",)),
        name="fox_sample",
    )(page_table.reshape(-1), q_bd, k_new, v_new, lf_new,
      *([cache_k] * n_pages), *([cache_v] * n_pages), *([cache_lf] * n_pages))


def _ab_out_ln_kernel(x_ref, oh_ref, of_ref, wh_ref, wf_ref, g_ref, b_ref, o_ref, *, alpha):
    y = (jnp.dot(oh_ref[...].astype(BF16), wh_ref[...], preferred_element_type=F32)
         + jnp.dot(of_ref[...].astype(BF16), wf_ref[...], preferred_element_type=F32))
    o_ref[...] = _layer_norm(alpha * x_ref[...] + y, g_ref[...], b_ref[...])


def _ab_out_ln(x, oh, of, wh, wf, g, b, alpha):
    n, d = x.shape
    tm = min(ROW_TILE, n)
    return pl.pallas_call(
        functools.partial(_ab_out_ln_kernel, alpha=alpha),
        grid=(n // tm,),
        in_specs=[_rows(tm, d), _rows(tm, oh.shape[1]), _rows(tm, of.shape[1]), _resident(wh.shape),
                  _resident(wf.shape), _resident((1, d)), _resident((1, d))],
        out_specs=_rows(tm, d),
        out_shape=jax.ShapeDtypeStruct((n, d), F32),
        compiler_params=_cparams(("parallel",)),
        name="ab_out_ln",
    )(x, oh, of, wh, wf, g, b)


def _gmlp_in_kernel(x_ref, w_ref, g_ref, b_ref, u_ref, v_ref):
    xb = x_ref[...].astype(BF16)
    half = u_ref.shape[1]
    u_ref[...] = jax.nn.gelu(jnp.dot(xb, w_ref[:, :half], preferred_element_type=F32))
    v = jax.nn.gelu(jnp.dot(xb, w_ref[:, half:], preferred_element_type=F32))
    v_ref[...] = _layer_norm(v, g_ref[...], b_ref[...])


def _gmlp_in(x, w, g, b):
    n, d = x.shape
    half = w.shape[1] // 2
    tm = min(ROW_TILE, n)
    return pl.pallas_call(
        _gmlp_in_kernel,
        grid=(n // tm,),
        in_specs=[_rows(tm, d), _resident(w.shape), _resident((1, half)), _resident((1, half))],
        out_specs=[_rows(tm, half), _rows(tm, half)],
        out_shape=[jax.ShapeDtypeStruct((n, half), F32)] * 2,
        compiler_params=_cparams(("parallel",)),
        name="gmlp_in",
    )(x, w, g, b)


def _gmlp_mix_out_ln_kernel(x_ref, u_ref, v_ref, ws_ref, bs_ref, wo_ref, g_ref, b_ref, o_ref, z_scr,
                            *, alpha, period):
    groups, ch, _ = ws_ref.shape
    gw = v_ref.shape[1] // groups
    t = lax.broadcasted_iota(jnp.int32, (ch, ch), 0)
    s = lax.broadcasted_iota(jnp.int32, (ch, ch), 1)
    causal = (t // period == s // period) & (s % period <= t % period)
    for g in range(groups):
        w_g = jnp.where(causal, ws_ref[g], 0.0).astype(BF16)
        cols = slice(g * gw, (g + 1) * gw)
        for c in range(x_ref.shape[0] // ch):
            rows = slice(c * ch, (c + 1) * ch)
            mixed = (jnp.dot(w_g, v_ref[rows, cols].astype(BF16), preferred_element_type=F32)
                     + bs_ref[:, g:g + 1])
            z_scr[rows, cols] = (u_ref[rows, cols] * mixed).astype(BF16)
    y = jnp.dot(z_scr[...], wo_ref[...], preferred_element_type=F32)
    o_ref[...] = _layer_norm(alpha * x_ref[...] + y, g_ref[...], b_ref[...])


def _gmlp_mix_out_ln(x, u, v, w_s, b_s_t, w_out, g, b, alpha, period):
    n, d = x.shape
    half = u.shape[1]
    tm = min(ROW_TILE, n)
    return pl.pallas_call(
        functools.partial(_gmlp_mix_out_ln_kernel, alpha=alpha, period=period),
        grid=(n // tm,),
        in_specs=[_rows(tm, d), _rows(tm, half), _rows(tm, half), _resident(w_s.shape),
                  _resident(b_s_t.shape), _resident(w_out.shape), _resident((1, d)), _resident((1, d))],
        out_specs=_rows(tm, d),
        out_shape=jax.ShapeDtypeStruct((n, d), F32),
        scratch_shapes=[pltpu.VMEM((tm, half), BF16)],
        compiler_params=_cparams(("parallel",)),
        name="gmlp_mix_out_ln",
    )(x, u, v, w_s, b_s_t, w_out, g, b)


def kernel(x_prompt, x_sample, cache_fox_k, cache_fox_v, cache_fox_logf, state_hg, page_table, ln_g, ln_b,
           ffn_w_gate, ffn_w_up, ffn_w_down, ab_w_in, hg_lb_logits, hg_norm_g, fox_f_bias, ab_w_out, c_w_in,
           c_ln_g, c_ln_b, c_w_s, c_b_s, c_w_out):
    nb_p, t_p, d = x_prompt.shape
    nb_s, t_s, _ = x_sample.shape
    depth = ln_g.shape[0]
    alpha = (2.0 * depth) ** 0.25
    hg_heads, hg_k, hg_v = state_hg.shape[2:]
    fox_heads, fox_hd = cache_fox_k.shape[3:]
    hw, fw = hg_heads * hg_k, fox_heads * fox_hd
    assert hg_k == hg_v and hw == hg_heads * hg_v

    xp = x_prompt.reshape(nb_p * t_p, d)
    xs = x_sample.reshape(nb_s * t_s, d)
    row = lambda a: a.reshape(1, -1)
    outs = {name: [] for name in ("k_p", "v_p", "lf_p", "hg_p", "k_s", "v_s", "lf_s", "hg_s", "cv_s")}

    for l in range(depth):
        j = l // 2

        def half_step(x, i):
            return _ffn_ln(x, ffn_w_gate[l, i].astype(BF16), ffn_w_up[l, i].astype(BF16),
                           ffn_w_down[l, i].astype(BF16), row(ln_g[l, i]), row(ln_b[l, i]), alpha)

        xp, xs = half_step(xp, 0), half_step(xs, 0)
        if l % 2 == 0:
            main_w = 4 * hw + 3 * fw
            w_main = ab_w_in[j, :, :main_w].astype(BF16)
            w_ff = jnp.pad(ab_w_in[j, :, main_w:], ((0, 0), (0, LANES - fox_heads))).astype(BF16)
            f_bias = jnp.pad(fox_f_bias[j], (0, LANES - fox_heads)).reshape(1, LANES)
            wh = ab_w_out[j, :hw].astype(BF16)
            wf = ab_w_out[j, hw:].astype(BF16)
            norm_g = row(hg_norm_g[j])

            def project(x):
                return _ab_in(x, w_main, w_ff, hg_lb_logits, f_bias, l, hw, fw, fox_heads, fox_hd ** -0.5)

            qh, kk, lg, ih, gt, fqb, fkb, fvb, fk, fv, lf, lft = project(xp)
            seq = lambda a: a.reshape(nb_p, t_p, a.shape[-1])
            o_h, s_t = _hgrn(seq(qh), seq(kk), seq(lg), seq(ih), seq(gt), norm_g, None, chunk=HG_CHUNK,
                             rows_per_step=HG_ROWS_PER_STEP, heads=hg_heads, dk=hg_k, dv=hg_v)
            fc = _lf_cumsum(lft, t_p)
            o_f = _fox_prompt(seq(fqb), seq(fkb), seq(fvb), fc, heads=fox_heads, hd=fox_hd)
            xp = _ab_out_ln(xp, o_h.reshape(-1, hw), o_f.reshape(-1, fw), wh, wf,
                            row(ln_g[l, 1]), row(ln_b[l, 1]), alpha)
            outs["k_p"].append(fk.reshape(nb_p, t_p, fox_heads, fox_hd))
            outs["v_p"].append(fv.reshape(nb_p, t_p, fox_heads, fox_hd))
            outs["lf_p"].append(lf.reshape(nb_p, t_p, fox_heads))
            outs["hg_p"].append(s_t)

            qh, kk, lg, ih, gt, fqb, fkb, fvb, fk, fv, lf, lft = project(xs)
            seq = lambda a: a.reshape(nb_s, t_s, a.shape[-1])
            o_h, s_t = _hgrn(seq(qh), seq(kk), seq(lg), seq(ih), seq(gt), norm_g, state_hg[j], chunk=t_s,
                             rows_per_step=t_s, heads=hg_heads, dk=hg_k, dv=hg_v)
            o_f = _fox_sample(seq(fqb), seq(fk), seq(fv), seq(lf), cache_fox_k[j], cache_fox_v[j],
                              cache_fox_logf[j], page_table, heads=fox_heads, hd=fox_hd)
            xs = _ab_out_ln(xs, o_h.reshape(-1, hw), o_f.reshape(-1, fw), wh, wf,
                            row(ln_g[l, 1]), row(ln_b[l, 1]), alpha)
            outs["k_s"].append(fk.reshape(nb_s, t_s, fox_heads, fox_hd))
            outs["v_s"].append(fv.reshape(nb_s, t_s, fox_heads, fox_hd))
            outs["lf_s"].append(lf.reshape(nb_s, t_s, fox_heads))
            outs["hg_s"].append(s_t)
        else:
            w_in = c_w_in[j].astype(BF16)
            w_out = c_w_out[j].astype(BF16)
            reps = C_CHUNK // t_s
            w_s_sample = jnp.tile(c_w_s[j][:, :t_s, :t_s], (1, reps, reps))
            b_s_sample = jnp.tile(c_b_s[j][:, :t_s], (1, reps))

            def mixer(x, w_s, b_s, period):
                u, v = _gmlp_in(x, w_in, row(c_ln_g[j]), row(c_ln_b[j]))
                y = _gmlp_mix_out_ln(x, u, v, w_s, b_s.T, w_out, row(ln_g[l, 1]), row(ln_b[l, 1]),
                                     alpha, period)
                return y, v

            xp, _ = mixer(xp, c_w_s[j], c_b_s[j], C_CHUNK)
            xs, v_rows = mixer(xs, w_s_sample, b_s_sample, t_s)
            outs["cv_s"].append(v_rows.reshape(nb_s, t_s, -1))
        xp, xs = half_step(xp, 2), half_step(xs, 2)

    stack = lambda name: jnp.stack(outs[name])
    return (xp.reshape(nb_p, t_p, d), xs.reshape(nb_s, t_s, d),
            stack("k_p"), stack("v_p"), stack("lf_p"), stack("hg_p"),
            stack("k_s"), stack("v_s"), stack("lf_s"), stack("hg_s"), stack("cv_s"))
```

```python
import functools

import jax
import jax.numpy as jnp
import numpy as np
from jax import lax
from jax.experimental import pallas as pl
from jax.experimental.pallas import tpu as pltpu

F32 = jnp.float32
BF16 = jnp.bfloat16
HIGHEST = lax.Precision.HIGHEST

NORM_EPS = 1e-5
LANES = 128
SUBLANES = 8
VMEM_LIMIT_BYTES = 56 * 1024 * 1024

ROW_TILE = 512
FFN_COL_CHUNK = 256
HG_SUB = SUBLANES
HG_CHUNK = 64
HG_ROWS_PER_STEP = 256
FOX_BQ = 1024
FOX_BK = 1024
FOX_SAMPLE_CHUNK_PAGES = 2
C_CHUNK = 128

_NT = (((1,), (1,)), ((), ()))
_TN = (((0,), (0,)), ((), ()))


def _cparams(semantics):
    return pltpu.CompilerParams(dimension_semantics=semantics, vmem_limit_bytes=VMEM_LIMIT_BYTES)


def _resident(shape):
    nd = len(shape)
    return pl.BlockSpec(shape, lambda *_: (0,) * nd, pipeline_mode=pl.Buffered(1))


def _rows(tm, width):
    return pl.BlockSpec((tm, width), lambda i: (i, 0))


def _layer_norm(y, g, b):
    mu = jnp.mean(y, axis=-1, keepdims=True)
    d = y - mu
    var = jnp.mean(d * d, axis=-1, keepdims=True)
    return d * lax.rsqrt(var + NORM_EPS) * g + b


def _silu(x):
    return x * jax.nn.sigmoid(x)


def _split_bf16(x, pieces=3):
    out = []
    for _ in range(pieces):
        piece = x.astype(BF16)
        out.append(piece)
        x = x - piece.astype(F32)
    return out


def _ffn_ln_kernel(x_ref, wg_ref, wu_ref, wd_ref, g_ref, b_ref, o_ref, a_scr, *, alpha):
    x = x_ref[...]
    xb = x.astype(BF16)
    hidden = wg_ref.shape[1]
    for c in range(hidden // FFN_COL_CHUNK):
        sl = slice(c * FFN_COL_CHUNK, (c + 1) * FFN_COL_CHUNK)
        gate = jnp.dot(xb, wg_ref[:, sl], preferred_element_type=F32)
        up = jnp.dot(xb, wu_ref[:, sl], preferred_element_type=F32)
        a_scr[:, sl] = (_silu(gate) * up).astype(BF16)
    y = jnp.dot(a_scr[...], wd_ref[...], preferred_element_type=F32)
    o_ref[...] = _layer_norm(alpha * x + 0.5 * y, g_ref[...], b_ref[...])


def _ffn_ln(x, wg, wu, wd, g, b, alpha):
    n, d = x.shape
    f = wg.shape[1]
    tm = min(ROW_TILE, n)
    return pl.pallas_call(
        functools.partial(_ffn_ln_kernel, alpha=alpha),
        grid=(n // tm,),
        in_specs=[_rows(tm, d), _resident((d, f)), _resident((d, f)), _resident((f, d)),
                  _resident((1, d)), _resident((1, d))],
        out_specs=_rows(tm, d),
        out_shape=jax.ShapeDtypeStruct((n, d), F32),
        scratch_shapes=[pltpu.VMEM((tm, f), BF16)],
        compiler_params=_cparams(("parallel",)),
        name="ffn_ln",
    )(x, wg, wu, wd, g, b)


def _ab_in_kernel(x_ref, w_ref, wff_ref, lbl_ref, fb_ref,
                  qh_ref, kk_ref, lg_ref, ih_ref, gt_ref, fqb_ref, fkb_ref, fvb_ref, fk_ref, fv_ref,
                  lf_ref, lft_ref, *, layer, hw, fw, fox_heads, fox_scale):
    xb = x_ref[...].astype(BF16)
    offs = np.cumsum([0, hw, hw, hw, hw, fw, fw, fw]).tolist()

    def seg(i):
        return jnp.dot(xb, w_ref[:, offs[i]:offs[i + 1]], preferred_element_type=F32)

    qh_ref[...] = _silu(seg(0))
    z = lbl_ref[...]
    e = jnp.exp(z - jnp.max(z, axis=0, keepdims=True))
    lb = jnp.sum(e[:layer + 1], axis=0, keepdims=True) / jnp.sum(e, axis=0, keepdims=True)
    f_gate = lb + (1.0 - lb) * jax.nn.sigmoid(seg(1))
    kk_ref[...] = 1.0 - f_gate
    lg_ref[...] = jnp.log(f_gate)
    ih_ref[...] = seg(2)
    gt_ref[...] = _silu(seg(3))
    fqb_ref[...] = (seg(4) * fox_scale).astype(BF16)
    fk = seg(5)
    fk_ref[...] = fk
    fkb_ref[...] = fk.astype(BF16)
    fv = seg(6)
    fv_ref[...] = fv
    fvb_ref[...] = fv.astype(BF16)
    ff = jnp.dot(xb, wff_ref[...], preferred_element_type=F32) + fb_ref[...]
    lf = jax.nn.log_sigmoid(ff)
    lf_ref[...] = lf[:, :fox_heads]
    lft_ref[...] = lf.T[:fox_heads, :]


def _ab_in(x, w_main, w_ff, lb_logits, f_bias, layer, hw, fw, fox_heads, fox_scale):
    n, d = x.shape
    tm = min(ROW_TILE, n)
    f32_out = lambda w: jax.ShapeDtypeStruct((n, w), F32)
    bf_out = lambda w: jax.ShapeDtypeStruct((n, w), BF16)
    out_shape = ([f32_out(hw)] * 5 + [bf_out(fw)] * 3 + [f32_out(fw)] * 2
                 + [f32_out(fox_heads), jax.ShapeDtypeStruct((fox_heads, n), F32)])
    out_specs = ([_rows(tm, hw)] * 5 + [_rows(tm, fw)] * 5
                 + [_rows(tm, fox_heads), pl.BlockSpec((fox_heads, tm), lambda i: (0, i))])
    return pl.pallas_call(
        functools.partial(_ab_in_kernel, layer=layer, hw=hw, fw=fw, fox_heads=fox_heads,
                          fox_scale=fox_scale),
        grid=(n // tm,),
        in_specs=[_rows(tm, d), _resident(w_main.shape), _resident(w_ff.shape),
                  _resident(lb_logits.shape), _resident(f_bias.shape)],
        out_specs=out_specs,
        out_shape=out_shape,
        compiler_params=_cparams(("parallel",)),
        name="ab_in",
    )(x, w_main, w_ff, lb_logits, f_bias)


def _hgrn_levels(chunk):
    levels, h = [], HG_SUB
    while h < chunk:
        levels.append(h)
        h *= 2
    return levels


def _hgrn_prefix_matrix(chunk):
    t = lax.broadcasted_iota(jnp.int32, (chunk, chunk), 0)
    s = lax.broadcasted_iota(jnp.int32, (chunk, chunk), 1)
    parts = [s <= t]
    for h in _hgrn_levels(chunk):
        ref_row = (t // (2 * h)) * (2 * h) + h - 1
        parts.append(s <= ref_row)
    return jnp.concatenate(parts, axis=0).astype(F32)


def _hgrn_diag(q, k, b, v):
    rows, dk = q.shape
    nb = rows // HG_SUB
    q3, k3, b3 = (a.reshape(nb, HG_SUB, dk) for a in (q, k, b))
    v3 = v.reshape(nb, HG_SUB, v.shape[-1])
    t_loc = lax.broadcasted_iota(jnp.int32, (1, HG_SUB, 1), 1)
    o3 = jnp.zeros(v3.shape, F32)
    for s in range(HG_SUB):
        decay = jnp.exp(jnp.where(t_loc >= s, b3 - b3[:, s:s + 1, :], -jnp.inf))
        a = jnp.sum(q3 * decay * k3[:, s:s + 1, :], axis=-1, keepdims=True)
        o3 = o3 + a * v3[:, s:s + 1, :]
    return o3.reshape(rows, v.shape[-1])


def _hgrn_kernel(*refs, chunk, heads, dk, dv, has_s0):
    if has_s0:
        q_ref, k_ref, g_ref, v_ref, gt_ref, ng_ref, s0_ref, o_ref, st_ref, s_scr = refs
    else:
        q_ref, k_ref, g_ref, v_ref, gt_ref, ng_ref, o_ref, st_ref, s_scr = refs
    tb = pl.program_id(1)

    @pl.when(tb == 0)
    def _():
        s_scr[...] = s0_ref[...] if has_s0 else jnp.zeros(s_scr.shape, F32)

    levels = _hgrn_levels(chunk)
    prefix = _hgrn_prefix_matrix(chunk)
    t = lax.broadcasted_iota(jnp.int32, (chunk, chunk), 0)
    s = lax.broadcasted_iota(jnp.int32, (chunk, chunk), 1)
    masks = [(t // (2 * h) == s // (2 * h)) & (t % (2 * h) >= h) & (s % (2 * h) < h) for h in levels]
    ng = ng_ref[...]

    for c in range(q_ref.shape[0] // chunk):
        rows = slice(c * chunk, (c + 1) * chunk)
        q, k, v = q_ref[rows, :], k_ref[rows, :], v_ref[rows, :]
        stacked = jnp.dot(prefix, g_ref[rows, :], precision=HIGHEST, preferred_element_type=F32)
        b = stacked[:chunk]
        b_last = b[chunk - 1:chunk, :]
        q_in = (q * jnp.exp(b)).astype(BF16)
        k_out = (k * jnp.exp(b_last - b)).astype(BF16)
        vb = v.astype(BF16)
        q_lv, k_lv = [], []
        for j in range(len(levels)):
            w = jnp.exp(-jnp.abs(b - stacked[(j + 1) * chunk:(j + 2) * chunk]))
            q_lv.append((q * w).astype(BF16))
            k_lv.append((k * w).astype(BF16))
        for hd in range(heads):
            ks = slice(hd * dk, (hd + 1) * dk)
            vs = slice(hd * dv, (hd + 1) * dv)
            state = s_scr[hd]
            o = jnp.dot(q_in[:, ks], state.astype(BF16), preferred_element_type=F32)
            o = o + _hgrn_diag(q[:, ks], k[:, ks], b[:, ks], v[:, vs])
            if levels:
                attn = jnp.zeros((chunk, chunk), F32)
                for j in range(len(levels)):
                    blk = lax.dot_general(q_lv[j][:, ks], k_lv[j][:, ks], _NT, preferred_element_type=F32)
                    attn = attn + jnp.where(masks[j], blk, 0.0)
                o = o + jnp.dot(attn.astype(BF16), vb[:, vs], preferred_element_type=F32)
            decay_col = jnp.broadcast_to(jnp.exp(b_last[:, ks]), (dv, dk)).T
            s_scr[hd] = decay_col * state + lax.dot_general(k_out[:, ks], vb[:, vs], _TN,
                                                            preferred_element_type=F32)
            o = o * lax.rsqrt(jnp.mean(o * o, axis=-1, keepdims=True) + NORM_EPS) * ng * gt_ref[rows, vs]
            o_ref[rows, vs] = o.astype(o_ref.dtype)

    @pl.when(tb == pl.num_programs(1) - 1)
    def _():
        st_ref[...] = s_scr[...]


def _hgrn(q, k, g, v, gate, norm_g, s0, *, chunk, rows_per_step, heads, dk, dv):
    nb, t, _ = q.shape
    tb = min(rows_per_step, t)
    seq = lambda w: pl.BlockSpec((None, tb, w), lambda b, i: (b, i, 0))
    st_spec = pl.BlockSpec((None, heads, dk, dv), lambda b, i: (b, 0, 0, 0))
    in_specs = [seq(heads * dk)] * 3 + [seq(heads * dv)] * 2 + [_resident(norm_g.shape)]
    args = [q, k, g, v, gate, norm_g]
    if s0 is not None:
        in_specs.append(st_spec)
        args.append(s0)
    return pl.pallas_call(
        functools.partial(_hgrn_kernel, chunk=chunk, heads=heads, dk=dk, dv=dv, has_s0=s0 is not None),
        grid=(nb, t // tb),
        in_specs=in_specs,
        out_specs=[seq(heads * dv), st_spec],
        out_shape=[jax.ShapeDtypeStruct((nb, t, heads * dv), BF16),
                   jax.ShapeDtypeStruct((nb, heads, dk, dv), F32)],
        scratch_shapes=[pltpu.VMEM((heads, dk, dv), F32)],
        compiler_params=_cparams(("parallel", "arbitrary")),
        name="hgrn_state" if s0 is not None else "hgrn_fresh",
    )(*args)


def _lf_cumsum_kernel(lft_ref, o_ref, x_scr, *, blocks_per_seq):
    heads, n = lft_ref.shape
    nblk = n // LANES
    for i in range(nblk):
        x_scr[i * heads:(i + 1) * heads, :] = lft_ref[:, i * LANES:(i + 1) * LANES]
    r = lax.broadcasted_iota(jnp.int32, (LANES, LANES), 0)
    c = lax.broadcasted_iota(jnp.int32, (LANES, LANES), 1)
    local = jnp.dot(x_scr[...], (r <= c).astype(F32), precision=HIGHEST, preferred_element_type=F32)
    rows = nblk * heads
    ri = lax.broadcasted_iota(jnp.int32, (rows, rows), 0)
    ci = lax.broadcasted_iota(jnp.int32, (rows, rows), 1)
    earlier = ((ri % heads == ci % heads) & (ci // heads < ri // heads)
               & (ci // (heads * blocks_per_seq) == ri // (heads * blocks_per_seq)))
    carry = jnp.dot(earlier.astype(F32), local, precision=HIGHEST, preferred_element_type=F32)
    x_scr[...] = local + carry[:, LANES - 1:LANES]
    for i in range(nblk):
        o_ref[:, i * LANES:(i + 1) * LANES] = x_scr[i * heads:(i + 1) * heads, :]


def _lf_cumsum(lft, seq_len):
    heads, n = lft.shape
    return pl.pallas_call(
        functools.partial(_lf_cumsum_kernel, blocks_per_seq=seq_len // LANES),
        out_shape=jax.ShapeDtypeStruct((heads, n), F32),
        scratch_shapes=[pltpu.VMEM((n // LANES * heads, LANES), F32)],
        compiler_params=pltpu.CompilerParams(vmem_limit_bytes=VMEM_LIMIT_BYTES),
        name="lf_cumsum",
    )(lft)


def _fox_prompt_kernel(qi_ref, ki_ref, q_ref, k_ref, v_ref, fk_ref, fq_ref, o_ref,
                       qm_scr, m_scr, l_scr, acc_scr, *, bq, bk, hd):
    step = pl.program_id(2)
    qi = qi_ref[step]
    ki = ki_ref[step]
    lane = lax.broadcasted_iota(jnp.int32, (1, 2 * hd), 1)

    @pl.when(ki == 0)
    def _():
        q = q_ref[...]
        for e in range(2):
            qm_scr[e] = jnp.where(lane // hd == e, q, jnp.zeros_like(q))
        m_scr[...] = jnp.full(m_scr.shape, -jnp.inf, F32)
        l_scr[...] = jnp.zeros(l_scr.shape, F32)
        acc_scr[...] = jnp.zeros(acc_scr.shape, F32)

    def update(masked):
        kb = k_ref[...]
        vb = v_ref[...]
        if masked:
            row = lax.broadcasted_iota(jnp.int32, (bq, bk), 0)
            col = lax.broadcasted_iota(jnp.int32, (bq, bk), 1)
            visible = (col - row) <= (qi * bq - ki * bk)
        for e in range(2):
            bias = fq_ref[e:e + 1, 0:1] - fk_ref[e:e + 1, :]
            s = lax.dot_general(qm_scr[e], kb, _NT, preferred_element_type=F32) + bias
            if masked:
                s = jnp.where(visible, s, -jnp.inf)
            m_prev = m_scr[e]
            m_new = jnp.maximum(m_prev, jnp.max(s, axis=-1, keepdims=True))
            alpha = jnp.exp(m_prev - m_new)
            p = jnp.exp(s - m_new)
            l_scr[e] = alpha * l_scr[e] + jnp.sum(p, axis=-1, keepdims=True)
            acc_scr[e] = alpha * acc_scr[e] + jnp.dot(p.astype(BF16), vb, preferred_element_type=F32)
            m_scr[e] = m_new

    overlaps_diagonal = (ki + 1) * bk - 1 > qi * bq

    @pl.when(overlaps_diagonal)
    def _():
        update(True)

    @pl.when(jnp.logical_not(overlaps_diagonal))
    def _():
        update(False)

    @pl.when(ki == ((qi + 1) * bq - 1) // bk)
    def _():
        o0 = acc_scr[0] / l_scr[0]
        o1 = acc_scr[1] / l_scr[1]
        o_ref[...] = jnp.where(lane // hd == 0, o0, o1).astype(o_ref.dtype)


def _fox_prompt(q, k, v, fc, *, heads, hd):
    nb, t, w = q.shape
    bq, bk = min(FOX_BQ, t), min(FOX_BK, t)
    pairs = [(i, j) for i in range(t // bq) for j in range(((i + 1) * bq - 1) // bk + 1)]
    qi_tab = jnp.asarray([p[0] for p in pairs], jnp.int32)
    ki_tab = jnp.asarray([p[1] for p in pairs], jnp.int32)
    nqb, nkb = t // bq, t // bk
    fc_pairs = fc.reshape(heads // 2, 2, nb * t)
    grid_spec = pltpu.PrefetchScalarGridSpec(
        num_scalar_prefetch=2,
        grid=(nb, heads // 2, len(pairs)),
        in_specs=[
            pl.BlockSpec((None, bq, 2 * hd), lambda b, hp, s, qi, ki: (b, qi[s], hp)),
            pl.BlockSpec((None, bk, 2 * hd), lambda b, hp, s, qi, ki: (b, ki[s], hp)),
            pl.BlockSpec((None, bk, 2 * hd), lambda b, hp, s, qi, ki: (b, ki[s], hp)),
            pl.BlockSpec((None, 2, bk), lambda b, hp, s, qi, ki: (hp, 0, b * nkb + ki[s])),
            pl.BlockSpec((None, 2, bq), lambda b, hp, s, qi, ki: (hp, 0, b * nqb + qi[s])),
        ],
        out_specs=pl.BlockSpec((None, bq, 2 * hd), lambda b, hp, s, qi, ki: (b, qi[s], hp)),
        scratch_shapes=[pltpu.VMEM((2, bq, 2 * hd), BF16), pltpu.VMEM((2, bq, 1), F32),
                        pltpu.VMEM((2, bq, 1), F32), pltpu.VMEM((2, bq, 2 * hd), F32)],
    )
    return pl.pallas_call(
        functools.partial(_fox_prompt_kernel, bq=bq, bk=bk, hd=hd),
        grid_spec=grid_spec,
        out_shape=jax.ShapeDtypeStruct((nb, t, w), BF16),
        compiler_params=_cparams(("parallel", "parallel", "arbitrary")),
        name="fox_prompt",
    )(qi_tab, ki_tab, q, k, v, fc_pairs, fc_pairs)


def _fox_sample_kernel(*refs, n_pages, heads, hd, page):
    pt_ref, qbd_ref, kn_ref, vn_ref, lfn_ref = refs[:5]
    kc_refs = refs[5:5 + n_pages]
    vc_refs = refs[5 + n_pages:5 + 2 * n_pages]
    lc_refs = refs[5 + 2 * n_pages:5 + 3 * n_pages]
    o_ref, kpad_scr, vpad_scr, lpad_scr, s_scr = refs[5 + 3 * n_pages:]
    del pt_ref
    tq = kn_ref.shape[0]

    kpad_scr[...] = jnp.zeros(kpad_scr.shape, F32)
    vpad_scr[...] = jnp.zeros(vpad_scr.shape, F32)
    lpad_scr[...] = jnp.zeros(lpad_scr.shape, F32)
    kpad_scr[0:tq, :] = kn_ref[...]
    vpad_scr[0:tq, :] = vn_ref[...]
    lpad_scr[0:tq, :] = lfn_ref[...]

    expand = (lax.broadcasted_iota(jnp.int32, (heads, LANES), 1) // tq
              == lax.broadcasted_iota(jnp.int32, (heads, LANES), 0)).astype(BF16)
    lower = (lax.broadcasted_iota(jnp.int32, (page, page), 1)
             <= lax.broadcasted_iota(jnp.int32, (page, page), 0)).astype(BF16)
    carry = jnp.zeros((1, LANES), F32)
    f_blocks = []
    for lf in [r[...] for r in lc_refs] + [lpad_scr[...]]:
        cum = carry
        for part in _split_bf16(lf):
            cols = jnp.dot(part, expand, preferred_element_type=F32).astype(BF16)
            cum = cum + jnp.dot(lower, cols, preferred_element_type=F32)
        carry = cum[page - 1:page, :]
        f_blocks.append(cum)

    def head_rows(page_refs, pad_scr, h, pages):
        pieces = [page_refs[j].reshape(page * heads, hd)[pl.ds(h, page, stride=heads), :] if j < n_pages
                  else pad_scr[:, h * hd:(h + 1) * hd] for j in pages]
        return jnp.concatenate(pieces, axis=0).astype(BF16)

    key_new = lax.broadcasted_iota(jnp.int32, (page, LANES), 0)
    query = lax.broadcasted_iota(jnp.int32, (page, LANES), 1) % tq
    col_max = jnp.full((1, LANES), -jnp.inf, F32)
    for first in range(0, n_pages + 1, FOX_SAMPLE_CHUNK_PAGES):
        pages = list(range(first, min(first + FOX_SAMPLE_CHUNK_PAGES, n_pages + 1)))
        s_c = None
        for h in range(heads):
            part = jnp.dot(head_rows(kc_refs, kpad_scr, h, pages), qbd_ref[h], preferred_element_type=F32)
            s_c = part if s_c is None else s_c + part
        s_c = s_c - jnp.concatenate([f_blocks[j] for j in pages], axis=0)
        if pages[-1] == n_pages:
            keep = len(pages) * page - page
            s_new = jnp.where((key_new < tq) & (key_new <= query), s_c[keep:], -jnp.inf)
            s_c = s_new if keep == 0 else jnp.concatenate([s_c[:keep], s_new], axis=0)
        s_scr[first * page:(pages[-1] + 1) * page, :] = s_c
        col_max = jnp.maximum(col_max, jnp.max(s_c, axis=0, keepdims=True))
    p = jnp.exp(s_scr[...] - col_max).T
    denom = jnp.sum(p, axis=1, keepdims=True)
    outs = []
    all_pages = list(range(n_pages + 1))
    for h in range(heads):
        rows = slice(h * tq, (h + 1) * tq)
        o_h = jnp.dot(p[rows].astype(BF16), head_rows(vc_refs, vpad_scr, h, all_pages),
                      preferred_element_type=F32)
        outs.append(o_h / denom[rows])
    o_ref[...] = jnp.concatenate(outs, axis=1)


def _fox_sample(q, k_new, v_new, lf_new, cache_k, cache_v, cache_lf, page_table, *, heads, hd):
    nb, tq, w = q.shape
    n_pages = page_table.shape[1]
    page = cache_k.shape[1]
    assert heads * tq <= LANES
    q_t = q.reshape(nb, tq, heads, hd).transpose(0, 2, 3, 1)
    q_bd = q_t[:, :, :, None, :] * jnp.eye(heads, dtype=q.dtype)[None, :, None, :, None]
    q_bd = jnp.pad(q_bd.reshape(nb, heads, hd, heads * tq), ((0, 0), (0, 0), (0, 0), (0, LANES - heads * tq)))
    new_spec = lambda width: pl.BlockSpec((None, tq, width), lambda b, pt: (b, 0, 0))
    kv_spec = lambda j: pl.BlockSpec((None, page, heads, hd), lambda b, pt: (pt[b * n_pages + j], 0, 0, 0))
    lf_spec = lambda j: pl.BlockSpec((None, page, heads), lambda b, pt: (pt[b * n_pages + j], 0, 0))
    grid_spec = pltpu.PrefetchScalarGridSpec(
        num_scalar_prefetch=1,
        grid=(nb,),
        in_specs=([pl.BlockSpec((None, heads, hd, LANES), lambda b, pt: (b, 0, 0, 0))]
                  + [new_spec(w)] * 2 + [new_spec(heads)]
                  + [kv_spec(j) for j in range(n_pages)] * 2 + [lf_spec(j) for j in range(n_pages)]),
        out_specs=new_spec(w),
        scratch_shapes=[pltpu.VMEM((page, w), F32), pltpu.VMEM((page, w), F32), pltpu.VMEM((page, heads), F32),
                        pltpu.VMEM(((n_pages + 1) * page, LANES), F32)],
    )
    return pl.pallas_call(
        functools.partial(_fox_sample_kernel, n_pages=n_pages, heads=heads, hd=hd, page=page),
        grid_spec=grid_spec,
        out_shape=jax.ShapeDtypeStruct((nb, tq, w), F32),
        compiler_params=_cparams(("parallel",)),
        name="fox_sample",
    )(page_table.reshape(-1), q_bd, k_new, v_new, lf_new,
      *([cache_k] * n_pages), *([cache_v] * n_pages), *([cache_lf] * n_pages))


def _fox_sample_t_kernel(*refs, n_pages, heads, hd, page):
    pt_ref, q_ref, kn_ref, vn_ref, lfn_ref = refs[:5]
    kc_refs = refs[5:5 + n_pages]
    vc_refs = refs[5 + n_pages:5 + 2 * n_pages]
    lc_refs = refs[5 + 2 * n_pages:5 + 3 * n_pages]
    o_ref, kpad_scr, vpad_scr, lpad_scr = refs[5 + 3 * n_pages:]
    del pt_ref
    tq = q_ref.shape[0]

    kpad_scr[...] = jnp.zeros(kpad_scr.shape, F32)
    vpad_scr[...] = jnp.zeros(vpad_scr.shape, F32)
    lpad_scr[...] = jnp.zeros(lpad_scr.shape, F32)
    kpad_scr[0:tq, :] = kn_ref[...]
    vpad_scr[0:tq, :] = vn_ref[...]
    lpad_scr[0:tq, :] = lfn_ref[...]

    upper = (lax.broadcasted_iota(jnp.int32, (page, page), 0)
             <= lax.broadcasted_iota(jnp.int32, (page, page), 1)).astype(BF16)
    eye = (lax.broadcasted_iota(jnp.int32, (heads, heads), 0)
           == lax.broadcasted_iota(jnp.int32, (heads, heads), 1)).astype(F32)

    def prefix(lf_t, carry):
        cum = carry
        for part in _split_bf16(lf_t):
            cum = cum + jnp.dot(part, upper, preferred_element_type=F32)
        return cum

    carry = jnp.zeros((heads, 1), F32)
    f_blocks = []
    for r in lc_refs:
        cum = prefix(r[...], carry)
        carry = cum[:, page - 1:page]
        f_blocks.append(cum)
    f_all = jnp.concatenate(f_blocks, axis=1)
    lf_new_t = lax.dot_general(eye, lpad_scr[...], _NT, precision=HIGHEST, preferred_element_type=F32)
    f_new = prefix(lf_new_t, carry)

    q_all = q_ref[...].astype(F32)
    t_q = lax.broadcasted_iota(jnp.int32, (tq, page), 0)
    s_k = lax.broadcasted_iota(jnp.int32, (tq, page), 1)
    visible = (s_k < tq) & (s_k <= t_q)
    outs = []
    for h in range(heads):
        cols = slice(h * hd, (h + 1) * hd)
        q_h = q_all[:, cols].astype(BF16)
        k_t = jnp.concatenate([r[h] for r in kc_refs], axis=1).astype(BF16)
        s = jnp.dot(q_h, k_t, preferred_element_type=F32) - f_all[h:h + 1, :]
        s_new = lax.dot_general(q_h, kpad_scr[:, cols].astype(BF16), _NT, preferred_element_type=F32)
        s_new = jnp.where(visible, s_new - f_new[h:h + 1, :], -jnp.inf)
        m = jnp.maximum(jnp.max(s, axis=-1, keepdims=True), jnp.max(s_new, axis=-1, keepdims=True))
        p = jnp.exp(s - m)
        p_new = jnp.exp(s_new - m)
        denom = jnp.sum(p, axis=-1, keepdims=True) + jnp.sum(p_new, axis=-1, keepdims=True)
        v_t = jnp.concatenate([r[h] for r in vc_refs], axis=1).astype(BF16)
        o_h = (lax.dot_general(p.astype(BF16), v_t, _NT, preferred_element_type=F32)
               + jnp.dot(p_new.astype(BF16), vpad_scr[:, cols].astype(BF16), preferred_element_type=F32))
        outs.append(o_h / denom)
    o_ref[...] = jnp.concatenate(outs, axis=1)


def _fox_sample_t(q, k_new, v_new, lf_new, cache_k, cache_v, cache_lf, page_table, *, heads, hd):
    nb, tq, w = q.shape
    n_pages = page_table.shape[1]
    page = cache_k.shape[1]
    k_t = jnp.transpose(cache_k, (0, 2, 3, 1))
    v_t = jnp.transpose(cache_v, (0, 2, 3, 1))
    lf_t = jnp.transpose(cache_lf, (0, 2, 1))
    new_spec = lambda width: pl.BlockSpec((None, tq, width), lambda b, pt: (b, 0, 0))
    kv_spec = lambda j: pl.BlockSpec((None, heads, hd, page), lambda b, pt: (pt[b * n_pages + j], 0, 0, 0))
    lf_spec = lambda j: pl.BlockSpec((None, heads, page), lambda b, pt: (pt[b * n_pages + j], 0, 0))
    grid_spec = pltpu.PrefetchScalarGridSpec(
        num_scalar_prefetch=1,
        grid=(nb,),
        in_specs=([new_spec(w)] * 3 + [new_spec(heads)]
                  + [kv_spec(j) for j in range(n_pages)] * 2 + [lf_spec(j) for j in range(n_pages)]),
        out_specs=new_spec(w),
        scratch_shapes=[pltpu.VMEM((page, w), F32), pltpu.VMEM((page, w), F32), pltpu.VMEM((page, heads), F32)],
    )
    return pl.pallas_call(
        functools.partial(_fox_sample_t_kernel, n_pages=n_pages, heads=heads, hd=hd, page=page),
        grid_spec=grid_spec,
        out_shape=jax.ShapeDtypeStruct((nb, tq, w), F32),
        compiler_params=_cparams(("parallel",)),
        name="fox_sample_t",
    )(page_table.reshape(-1), q, k_new, v_new, lf_new,
      *([k_t] * n_pages), *([v_t] * n_pages), *([lf_t] * n_pages))


def _ab_out_ln_kernel(x_ref, oh_ref, of_ref, wh_ref, wf_ref, g_ref, b_ref, o_ref, *, alpha):
    y = (jnp.dot(oh_ref[...].astype(BF16), wh_ref[...], preferred_element_type=F32)
         + jnp.dot(of_ref[...].astype(BF16), wf_ref[...], preferred_element_type=F32))
    o_ref[...] = _layer_norm(alpha * x_ref[...] + y, g_ref[...], b_ref[...])


def _ab_out_ln(x, oh, of, wh, wf, g, b, alpha):
    n, d = x.shape
    tm = min(ROW_TILE, n)
    return pl.pallas_call(
        functools.partial(_ab_out_ln_kernel, alpha=alpha),
        grid=(n // tm,),
        in_specs=[_rows(tm, d), _rows(tm, oh.shape[1]), _rows(tm, of.shape[1]), _resident(wh.shape),
                  _resident(wf.shape), _resident((1, d)), _resident((1, d))],
        out_specs=_rows(tm, d),
        out_shape=jax.ShapeDtypeStruct((n, d), F32),
        compiler_params=_cparams(("parallel",)),
        name="ab_out_ln",
    )(x, oh, of, wh, wf, g, b)


def _gmlp_in_kernel(x_ref, w_ref, g_ref, b_ref, u_ref, v_ref):
    xb = x_ref[...].astype(BF16)
    half = u_ref.shape[1]
    u_ref[...] = jax.nn.gelu(jnp.dot(xb, w_ref[:, :half], preferred_element_type=F32))
    v = jax.nn.gelu(jnp.dot(xb, w_ref[:, half:], preferred_element_type=F32))
    v_ref[...] = _layer_norm(v, g_ref[...], b_ref[...])


def _gmlp_in(x, w, g, b):
    n, d = x.shape
    half = w.shape[1] // 2
    tm = min(ROW_TILE, n)
    return pl.pallas_call(
        _gmlp_in_kernel,
        grid=(n // tm,),
        in_specs=[_rows(tm, d), _resident(w.shape), _resident((1, half)), _resident((1, half))],
        out_specs=[_rows(tm, half), _rows(tm, half)],
        out_shape=[jax.ShapeDtypeStruct((n, half), F32)] * 2,
        compiler_params=_cparams(("parallel",)),
        name="gmlp_in",
    )(x, w, g, b)


def _gmlp_mix_out_ln_kernel(x_ref, u_ref, v_ref, ws_ref, bs_ref, wo_ref, g_ref, b_ref, o_ref, z_scr,
                            *, alpha, period):
    groups, ch, _ = ws_ref.shape
    gw = v_ref.shape[1] // groups
    t = lax.broadcasted_iota(jnp.int32, (ch, ch), 0)
    s = lax.broadcasted_iota(jnp.int32, (ch, ch), 1)
    causal = (t // period == s // period) & (s % period <= t % period)
    for g in range(groups):
        w_g = jnp.where(causal, ws_ref[g], 0.0).astype(BF16)
        cols = slice(g * gw, (g + 1) * gw)
        for c in range(x_ref.shape[0] // ch):
            rows = slice(c * ch, (c + 1) * ch)
            mixed = (jnp.dot(w_g, v_ref[rows, cols].astype(BF16), preferred_element_type=F32)
                     + bs_ref[:, g:g + 1])
            z_scr[rows, cols] = (u_ref[rows, cols] * mixed).astype(BF16)
    y = jnp.dot(z_scr[...], wo_ref[...], preferred_element_type=F32)
    o_ref[...] = _layer_norm(alpha * x_ref[...] + y, g_ref[...], b_ref[...])


def _gmlp_mix_out_ln(x, u, v, w_s, b_s_t, w_out, g, b, alpha, period):
    n, d = x.shape
    half = u.shape[1]
    tm = min(ROW_TILE, n)
    return pl.pallas_call(
        functools.partial(_gmlp_mix_out_ln_kernel, alpha=alpha, period=period),
        grid=(n // tm,),
        in_specs=[_rows(tm, d), _rows(tm, half), _rows(tm, half), _resident(w_s.shape),
                  _resident(b_s_t.shape), _resident(w_out.shape), _resident((1, d)), _resident((1, d))],
        out_specs=_rows(tm, d),
        out_shape=jax.ShapeDtypeStruct((n, d), F32),
        scratch_shapes=[pltpu.VMEM((tm, half), BF16)],
        compiler_params=_cparams(("parallel",)),
        name="gmlp_mix_out_ln",
    )(x, u, v, w_s, b_s_t, w_out, g, b)


def kernel(x_prompt, x_sample, cache_fox_k, cache_fox_v, cache_fox_logf, state_hg, page_table, ln_g, ln_b,
           ffn_w_gate, ffn_w_up, ffn_w_down, ab_w_in, hg_lb_logits, hg_norm_g, fox_f_bias, ab_w_out, c_w_in,
           c_ln_g, c_ln_b, c_w_s, c_b_s, c_w_out):
    nb_p, t_p, d = x_prompt.shape
    nb_s, t_s, _ = x_sample.shape
    depth = ln_g.shape[0]
    alpha = (2.0 * depth) ** 0.25
    hg_heads, hg_k, hg_v = state_hg.shape[2:]
    fox_heads, fox_hd = cache_fox_k.shape[3:]
    hw, fw = hg_heads * hg_k, fox_heads * fox_hd
    assert hg_k == hg_v and hw == hg_heads * hg_v

    xp = x_prompt.reshape(nb_p * t_p, d)
    xs = x_sample.reshape(nb_s * t_s, d)
    row = lambda a: a.reshape(1, -1)
    outs = {name: [] for name in ("k_p", "v_p", "lf_p", "hg_p", "k_s", "v_s", "lf_s", "hg_s", "cv_s")}

    for l in range(depth):
        j = l // 2

        def half_step(x, i):
            return _ffn_ln(x, ffn_w_gate[l, i].astype(BF16), ffn_w_up[l, i].astype(BF16),
                           ffn_w_down[l, i].astype(BF16), row(ln_g[l, i]), row(ln_b[l, i]), alpha)

        xp, xs = half_step(xp, 0), half_step(xs, 0)
        if l % 2 == 0:
            main_w = 4 * hw + 3 * fw
            w_main = ab_w_in[j, :, :main_w].astype(BF16)
            w_ff = jnp.pad(ab_w_in[j, :, main_w:], ((0, 0), (0, LANES - fox_heads))).astype(BF16)
            f_bias = jnp.pad(fox_f_bias[j], (0, LANES - fox_heads)).reshape(1, LANES)
            wh = ab_w_out[j, :hw].astype(BF16)
            wf = ab_w_out[j, hw:].astype(BF16)
            norm_g = row(hg_norm_g[j])

            def project(x):
                return _ab_in(x, w_main, w_ff, hg_lb_logits, f_bias, l, hw, fw, fox_heads, fox_hd ** -0.5)

            qh, kk, lg, ih, gt, fqb, fkb, fvb, fk, fv, lf, lft = project(xp)
            seq = lambda a: a.reshape(nb_p, t_p, a.shape[-1])
            o_h, s_t = _hgrn(seq(qh), seq(kk), seq(lg), seq(ih), seq(gt), norm_g, None, chunk=HG_CHUNK,
                             rows_per_step=HG_ROWS_PER_STEP, heads=hg_heads, dk=hg_k, dv=hg_v)
            fc = _lf_cumsum(lft, t_p)
            o_f = _fox_prompt(seq(fqb), seq(fkb), seq(fvb), fc, heads=fox_heads, hd=fox_hd)
            xp = _ab_out_ln(xp, o_h.reshape(-1, hw), o_f.reshape(-1, fw), wh, wf,
                            row(ln_g[l, 1]), row(ln_b[l, 1]), alpha)
            outs["k_p"].append(fk.reshape(nb_p, t_p, fox_heads, fox_hd))
            outs["v_p"].append(fv.reshape(nb_p, t_p, fox_heads, fox_hd))
            outs["lf_p"].append(lf.reshape(nb_p, t_p, fox_heads))
            outs["hg_p"].append(s_t)

            qh, kk, lg, ih, gt, fqb, fkb, fvb, fk, fv, lf, lft = project(xs)
            seq = lambda a: a.reshape(nb_s, t_s, a.shape[-1])
            o_h, s_t = _hgrn(seq(qh), seq(kk), seq(lg), seq(ih), seq(gt), norm_g, state_hg[j], chunk=t_s,
                             rows_per_step=t_s, heads=hg_heads, dk=hg_k, dv=hg_v)
            o_f = _fox_sample_t(seq(fqb), seq(fk), seq(fv), seq(lf), cache_fox_k[j], cache_fox_v[j],
                                cache_fox_logf[j], page_table, heads=fox_heads, hd=fox_hd)
            xs = _ab_out_ln(xs, o_h.reshape(-1, hw), o_f.reshape(-1, fw), wh, wf,
                            row(ln_g[l, 1]), row(ln_b[l, 1]), alpha)
            outs["k_s"].append(fk.reshape(nb_s, t_s, fox_heads, fox_hd))
            outs["v_s"].append(fv.reshape(nb_s, t_s, fox_heads, fox_hd))
            outs["lf_s"].append(lf.reshape(nb_s, t_s, fox_heads))
            outs["hg_s"].append(s_t)
        else:
            w_in = c_w_in[j].astype(BF16)
            w_out = c_w_out[j].astype(BF16)
            reps = C_CHUNK // t_s
            w_s_sample = jnp.tile(c_w_s[j][:, :t_s, :t_s], (1, reps, reps))
            b_s_sample = jnp.tile(c_b_s[j][:, :t_s], (1, reps))

            def mixer(x, w_s, b_s, period):
                u, v = _gmlp_in(x, w_in, row(c_ln_g[j]), row(c_ln_b[j]))
                y = _gmlp_mix_out_ln(x, u, v, w_s, b_s.T, w_out, row(ln_g[l, 1]), row(ln_b[l, 1]),
                                     alpha, period)
                return y, v

            xp, _ = mixer(xp, c_w_s[j], c_b_s[j], C_CHUNK)
            xs, v_rows = mixer(xs, w_s_sample, b_s_sample, t_s)
            outs["cv_s"].append(v_rows.reshape(nb_s, t_s, -1))
        xp, xs = half_step(xp, 2), half_step(xs, 2)

    stack = lambda name: jnp.stack(outs[name])
    return (xp.reshape(nb_p, t_p, d), xs.reshape(nb_s, t_s, d),
            stack("k_p"), stack("v_p"), stack("lf_p"), stack("hg_p"),
            stack("k_s"), stack("v_s"), stack("lf_s"), stack("hg_s"), stack("cv_s"))
```
